```python
import math
import jax
import jax.numpy as jnp
from jax import lax
import numpy as np

D_MODEL = 4096
BATCH = 2
SEQ = 8192
DEPTH = 2

HEAD_DIM = 128
ROPE_THETA = 500000.0
PARTIAL_ROPE_DIM = HEAD_DIM // 4
A_HEADS = 12
A_PATTERNS = ((128, 1), (512, 4), (2048, 16))
B_HEADS = 8
Q_LORA = 1536
KV_LORA = 512
QK_NOPE = 128
QK_ROPE = 64
V_DIM = 128
C_HEADS = 12
QBLOCK = 128
N_BRANCH = 3
D_FF = -(-(8 * D_MODEL) // (3 * 256)) * 256
DEEPNORM_ALPHA = (2.0 * DEPTH) ** 0.25
DEEPNORM_BETA = (8.0 * DEPTH) ** -0.25
NEG = -1e30
A_WIDTH = A_HEADS * HEAD_DIM
B_WIDTH = B_HEADS * V_DIM
C_WIDTH = C_HEADS * HEAD_DIM
IN_WIDTHS = (A_WIDTH, A_WIDTH, A_WIDTH,
             Q_LORA, KV_LORA, QK_ROPE,
             C_WIDTH, C_WIDTH, C_WIDTH, C_HEADS,
             N_BRANCH * D_MODEL)
IN_COLS = sum(IN_WIDTHS)
IN_SPLITS = tuple(int(v) for v in np.cumsum(IN_WIDTHS)[:-1])

kernel_name = "hybrid_dilated_mla_fox_deepnorm_adaln"


def _layer_norm(x, g, b, eps=1e-5):
    xf = x.astype(jnp.float32)
    mu = jnp.mean(xf, axis=-1, keepdims=True)
    var = jnp.mean(jnp.square(xf - mu), axis=-1, keepdims=True)
    y = (xf - mu) * lax.rsqrt(var + eps) * g.astype(jnp.float32) + b.astype(jnp.float32)
    return y.astype(x.dtype)


def _rms_norm(x, g, eps=1e-6):
    xf = x.astype(jnp.float32)
    y = xf * lax.rsqrt(jnp.mean(jnp.square(xf), axis=-1, keepdims=True) + eps) * g.astype(jnp.float32)
    return y.astype(x.dtype)


def _rotary(x, positions, rot_dim):
    half = rot_dim // 2
    inv_freq = jnp.exp(-math.log(ROPE_THETA) * jnp.arange(half, dtype=jnp.float32) * (2.0 / rot_dim))
    ang = positions.astype(jnp.float32)[..., None] * inv_freq
    ang = ang.reshape(ang.shape[:2] + (1,) * (x.ndim - 3) + (half,))
    cos, sin = jnp.cos(ang), jnp.sin(ang)
    xr = x[..., :rot_dim].astype(jnp.float32)
    x1, x2 = xr[..., :half], xr[..., half:]
    rot = jnp.concatenate([x1 * cos - x2 * sin, x2 * cos + x1 * sin], axis=-1)
    return jnp.concatenate([rot.astype(x.dtype), x[..., rot_dim:]], axis=-1)


def _dilated_group(q, k, v, window, dilation):
    B, S, H, D = q.shape
    blk = window // dilation
    span = blk * dilation
    s_pad = -(-S // span) * span
    nb = s_pad // span

    def strided(t):
        t = jnp.pad(t, ((0, 0), (0, s_pad - S), (0, 0), (0, 0)))
        t = t.reshape(B, s_pad // dilation, dilation, H, D).swapaxes(1, 2)
        return t.reshape(B, dilation, nb, blk, H, D)

    def band(t):
        prev = jnp.concatenate([jnp.zeros_like(t[:, :, :1]), t[:, :, :-1]], axis=2)
        return jnp.concatenate([prev, t], axis=3)

    qs = strided(q).astype(jnp.float32)
    ks = band(strided(k)).astype(jnp.float32)
    vs = band(strided(v)).astype(jnp.float32)
    s = jnp.einsum("brnqhd,brnkhd->brnhqk", qs, ks) * (D ** -0.5)
    a = jnp.arange(blk)[:, None]
    kk = jnp.arange(2 * blk)[None, :]
    in_window = (kk >= a) & (kk <= a + blk)
    has_prev = (jnp.arange(nb) > 0)[:, None, None] | (kk >= blk)[None]
    mask = (in_window[None] & has_prev)[None, None, :, None]
    s = jnp.where(mask, s, NEG)
    lse = jax.nn.logsumexp(s, axis=-1)
    p = jnp.exp(s - lse[..., None])
    o = jnp.einsum("brnhqk,brnkhd->brnqhd", p, vs)

    def unstride(t):
        rest = t.shape[4:]
        t = t.reshape((B, dilation, s_pad // dilation) + rest).swapaxes(1, 2)
        return t.reshape((B, s_pad) + rest)[:, :S]

    return unstride(o), unstride(lse.transpose(0, 1, 2, 4, 3))


def _dilated_attention(q, k, v):
    outs, lses = [], []
    for window, dilation in A_PATTERNS:
        o, l = _dilated_group(q, k, v, window, dilation)
        outs.append(o)
        lses.append(l)
    w = jax.nn.softmax(jnp.stack(lses, axis=0), axis=0)
    return jnp.einsum("gbsh,gbshd->bshd", w, jnp.stack(outs, axis=0)).astype(v.dtype)


def _causal_blocked_attention(q, k, v, log_f_cum=None):
    B, S, H, Dq = q.shape
    Dv = v.shape[-1]
    nb = S // QBLOCK
    scale = Dq ** -0.5
    kf = k.astype(jnp.float32)
    vf = v.astype(jnp.float32)
    key_pos = jnp.arange(S)
    qb = q.astype(jnp.float32).reshape(B, nb, QBLOCK, H, Dq).swapaxes(0, 1)
    if log_f_cum is None:
        kcum = None
        xs = (jnp.arange(nb), qb)
    else:
        kcum = log_f_cum.transpose(0, 2, 1)
        cb = log_f_cum.reshape(B, nb, QBLOCK, H).transpose(1, 0, 3, 2)
        xs = (jnp.arange(nb), qb, cb)

    def one_block(xs_n):
        n, qn = xs_n[0], xs_n[1]
        s = jnp.einsum("bqhd,bkhd->bhqk", qn, kf) * scale
        if kcum is not None:
            s = s + xs_n[2][..., None] - kcum[:, :, None, :]
        q_pos = n * QBLOCK + jnp.arange(QBLOCK)
        s = jnp.where(key_pos[None, :] <= q_pos[:, None], s, NEG)
        p = jax.nn.softmax(s, axis=-1)
        return jnp.einsum("bhqk,bkhd->bqhd", p, vf)

    o = lax.map(one_block, xs)
    return o.swapaxes(0, 1).reshape(B, S, H, Dv).astype(v.dtype)


def _hybrid_mixer(h, positions, w_in, b_f, g_qn, w_uq, g_kvn, w_ukv, w_br_a, w_br_b, w_br_c, w_o):
    B, S, _ = h.shape
    qa, ka, va, cq, ckv, kr, qc, kc, vc, f_logit, gate_logit = jnp.split(h @ w_in, IN_SPLITS, axis=-1)

    def heads(t, n):
        return t.reshape(B, S, n, -1)

    qa = _rotary(heads(qa, A_HEADS), positions, PARTIAL_ROPE_DIM)
    ka = _rotary(heads(ka, A_HEADS), positions, PARTIAL_ROPE_DIM)
    ya = _dilated_attention(qa, ka, heads(va, A_HEADS)).reshape(B, S, A_WIDTH) @ w_br_a

    qb = heads(_rms_norm(cq, g_qn) @ w_uq, B_HEADS)
    qb = jnp.concatenate([qb[..., :QK_NOPE], _rotary(qb[..., QK_NOPE:], positions, QK_ROPE)], axis=-1)
    kvb = heads(_rms_norm(ckv, g_kvn) @ w_ukv, B_HEADS)
    kr = jnp.broadcast_to(_rotary(kr[:, :, None, :], positions, QK_ROPE), (B, S, B_HEADS, QK_ROPE))
    kb = jnp.concatenate([kvb[..., :QK_NOPE], kr], axis=-1)
    yb = _causal_blocked_attention(qb, kb, kvb[..., QK_NOPE:]).reshape(B, S, B_WIDTH) @ w_br_b

    log_f = jax.nn.log_sigmoid(f_logit.astype(jnp.float32) + b_f.astype(jnp.float32))
    f_cum = jnp.cumsum(log_f, axis=1)
    yc = _causal_blocked_attention(heads(qc, C_HEADS), heads(kc, C_HEADS), heads(vc, C_HEADS), f_cum)
    yc = yc.reshape(B, S, C_WIDTH) @ w_br_c

    g = jax.nn.sigmoid(gate_logit.astype(jnp.float32)).reshape(B, S, N_BRANCH, D_MODEL)
    merged = (g[:, :, 0] * ya + g[:, :, 1] * yb + g[:, :, 2] * yc).astype(h.dtype)
    return merged @ w_o


def _swiglu(h, w_ffn_in, w_ffn_out):
    a, b = jnp.split(h @ w_ffn_in, 2, axis=-1)
    return (jax.nn.silu(a) * b) @ w_ffn_out


def setup_inputs(seed: int = 0) -> dict:
    key = jax.random.key(seed)
    ks = jax.random.split(key, 24)

    def nrm(k, shape, scale):
        return jax.random.normal(k, shape, jnp.float32) * scale

    x = nrm(ks[0], (BATCH, SEQ, D_MODEL), 1.0)
    c = nrm(ks[1], (BATCH, D_MODEL), 1.0)
    positions = (jax.random.randint(ks[2], (BATCH, 1), 0, 1024, jnp.int32)
                 + jnp.arange(SEQ, dtype=jnp.int32)[None, :])
    w_ada = nrm(ks[3], (DEPTH, D_MODEL, 6 * D_MODEL), 0.1 * D_MODEL ** -0.5)
    b_ada = nrm(ks[4], (DEPTH, 6 * D_MODEL), 0.01)
    w_in = nrm(ks[5], (DEPTH, D_MODEL, IN_COLS), D_MODEL ** -0.5)
    b_f = 2.0 + nrm(ks[6], (DEPTH, C_HEADS), 0.5)
    g_qn = 1.0 + nrm(ks[7], (DEPTH, Q_LORA), 0.01)
    w_uq = nrm(ks[8], (DEPTH, Q_LORA, B_HEADS * (QK_NOPE + QK_ROPE)), Q_LORA ** -0.5)
    g_kvn = 1.0 + nrm(ks[9], (DEPTH, KV_LORA), 0.01)
    w_ukv = nrm(ks[10], (DEPTH, KV_LORA, B_HEADS * (QK_NOPE + V_DIM)), KV_LORA ** -0.5)
    w_br_a = nrm(ks[11], (DEPTH, A_WIDTH, D_MODEL), A_WIDTH ** -0.5)
    w_br_b = nrm(ks[12], (DEPTH, B_WIDTH, D_MODEL), B_WIDTH ** -0.5)
    w_br_c = nrm(ks[13], (DEPTH, C_WIDTH, D_MODEL), C_WIDTH ** -0.5)
    w_o = nrm(ks[14], (DEPTH, D_MODEL, D_MODEL), DEEPNORM_BETA * D_MODEL ** -0.5)
    ln1_g = 1.0 + nrm(ks[15], (DEPTH, D_MODEL), 0.01)
    ln1_b = nrm(ks[16], (DEPTH, D_MODEL), 0.01)
    w_ffn_in = nrm(ks[17], (DEPTH, D_MODEL, 2 * D_FF), D_MODEL ** -0.5)
    w_ffn_out = nrm(ks[18], (DEPTH, D_FF, D_MODEL), DEEPNORM_BETA * D_FF ** -0.5)
    ln2_g = 1.0 + nrm(ks[19], (DEPTH, D_MODEL), 0.01)
    ln2_b = nrm(ks[20], (DEPTH, D_MODEL), 0.01)
    return {"x": x, "c": c, "positions": positions, "w_ada": w_ada, "b_ada": b_ada,
            "w_in": w_in, "b_f": b_f, "g_qn": g_qn, "w_uq": w_uq, "g_kvn": g_kvn,
            "w_ukv": w_ukv, "w_br_a": w_br_a, "w_br_b": w_br_b, "w_br_c": w_br_c,
            "w_o": w_o, "ln1_g": ln1_g, "ln1_b": ln1_b, "w_ffn_in": w_ffn_in,
            "w_ffn_out": w_ffn_out, "ln2_g": ln2_g, "ln2_b": ln2_b}


def reference(x, c, positions, w_ada, b_ada, w_in, b_f, g_qn, w_uq, g_kvn, w_ukv,
              w_br_a, w_br_b, w_br_c, w_o, ln1_g, ln1_b, w_ffn_in, w_ffn_out, ln2_g, ln2_b):
    for l in range(DEPTH):
        ada = jax.nn.silu(c) @ w_ada[l] + b_ada[l]
        sh1, sc1, gt1, sh2, sc2, gt2 = [t[:, None, :] for t in jnp.split(ada, 6, axis=-1)]
        h = x * (1.0 + sc1) + sh1
        y = _hybrid_mixer(h, positions, w_in[l], b_f[l], g_qn[l], w_uq[l], g_kvn[l], w_ukv[l],
                          w_br_a[l], w_br_b[l], w_br_c[l], w_o[l])
        x = _layer_norm(DEEPNORM_ALPHA * x + (1.0 + gt1) * y, ln1_g[l], ln1_b[l])
        h = x * (1.0 + sc2) + sh2
        y = _swiglu(h, w_ffn_in[l], w_ffn_out[l])
        x = _layer_norm(DEEPNORM_ALPHA * x + (1.0 + gt2) * y, ln2_g[l], ln2_b[l])
    return x
```

```python
import functools
import math

import jax
import jax.numpy as jnp
import numpy as np
from jax import lax
from jax.experimental import pallas as pl
from jax.experimental.pallas import tpu as pltpu

HEAD_DIM = 128
ROPE_THETA = 500000.0
PARTIAL_ROPE_DIM = HEAD_DIM // 4
A_HEADS = 12
A_PATTERNS = ((128, 1), (512, 4), (2048, 16))
B_HEADS = 8
QK_NOPE = 128
QK_ROPE = 64
V_DIM = 128
C_HEADS = 12
N_BRANCH = 3
A_WIDTH = A_HEADS * HEAD_DIM
B_WIDTH = B_HEADS * V_DIM
C_WIDTH = C_HEADS * HEAD_DIM
B_QK_PAD = 256
F_ROWS = 16
NEG = -1e30
LOG2E = math.log2(math.e)
LANES = 128
V7X_VMEM_CAP = 60 * 1024 * 1024

BF16 = jnp.bfloat16
F32 = jnp.float32


def _cparams(semantics, vmem_estimate):
    limit = int(min(max(vmem_estimate * 5 // 4, 32 * 1024 * 1024), V7X_VMEM_CAP))
    return pltpu.CompilerParams(dimension_semantics=semantics, vmem_limit_bytes=limit)


def _tile(n, prefs):
    for p in prefs:
        if n % p == 0:
            return p
    return n


def _resident(shape):
    return pl.BlockSpec(shape, lambda i: (0,) * len(shape), pipeline_mode=pl.Buffered(1))


def _dot(a, b):
    return jnp.dot(a, b, preferred_element_type=F32)


def _dot_nt(a, b):
    return lax.dot_general(a, b, (((1,), (1,)), ((), ())), preferred_element_type=F32)


def _rotate(t, cos, sin_lo, sin_hi, half):
    return t * cos + pltpu.roll(t, half, 1) * sin_hi + pltpu.roll(t, LANES - half, 1) * sin_lo


def _ada_kernel(c_ref, w_ref, b_ref, o_ref):
    cv = c_ref[...]
    s = (cv * jax.nn.sigmoid(cv)).astype(BF16)
    o_ref[...] = _dot(s, w_ref[...].astype(BF16)) + b_ref[...]


def _ada(c, w_ada, b_ada):
    depth, d, n = w_ada.shape
    b = c.shape[0]
    bn = _tile(n, (1024, 512, 256, 128))
    return pl.pallas_call(
        _ada_kernel,
        name="ada",
        grid=(depth, n // bn),
        in_specs=[pl.BlockSpec((b, d), lambda l, j: (0, 0)),
                  pl.BlockSpec((None, d, bn), lambda l, j: (l, 0, j)),
                  pl.BlockSpec((None, 1, bn), lambda l, j: (l, 0, j))],
        out_specs=pl.BlockSpec((None, b, bn), lambda l, j: (l, 0, j)),
        out_shape=jax.ShapeDtypeStruct((depth, b, n), F32),
        compiler_params=_cparams(("arbitrary", "arbitrary"), 2 * d * bn * 4 + d * bn * 2),
    )(c, w_ada, b_ada.reshape(depth, 1, n))


def _rope_table_kernel(pos_ref, freq_ref, mc_ref, m1_ref, mlo_ref, mhi_ref, cos_ref, lo_ref, hi_ref):
    ang = pos_ref[...] * freq_ref[...]
    cs = jnp.cos(ang)
    sn = jnp.sin(ang)
    cos_ref[...] = cs * mc_ref[...] + m1_ref[...]
    lo_ref[...] = sn * mlo_ref[...]
    hi_ref[...] = sn * mhi_ref[...]


def _rope_tables(pos_col, rot_dim):
    t = pos_col.shape[0]
    half = rot_dim // 2
    inv = np.exp(-math.log(ROPE_THETA) * np.arange(half, dtype=np.float32) * np.float32(2.0 / rot_dim))
    lane = np.arange(LANES)
    freq = np.where(lane < rot_dim, inv[lane % half], 0.0).astype(np.float32)[None]
    m_cos = (lane < rot_dim).astype(np.float32)[None]
    m_one = (lane >= rot_dim).astype(np.float32)[None]
    m_lo = np.where(lane < half, -1.0, 0.0).astype(np.float32)[None]
    m_hi = np.where((lane >= half) & (lane < rot_dim), 1.0, 0.0).astype(np.float32)[None]
    bm = _tile(t, (2048, 1024, 512, 256, 128))
    row = pl.BlockSpec((1, LANES), lambda i: (0, 0))
    tab = pl.BlockSpec((bm, LANES), lambda i: (i, 0))
    shp = jax.ShapeDtypeStruct((t, LANES), F32)
    return pl.pallas_call(
        _rope_table_kernel,
        name="rope_tables",
        grid=(t // bm,),
        in_specs=[pl.BlockSpec((bm, 1), lambda i: (i, 0)), row, row, row, row, row],
        out_specs=[tab, tab, tab],
        out_shape=[shp, shp, shp],
        compiler_params=_cparams(("arbitrary",), 16 * bm * LANES * 4),
    )(pos_col, jnp.asarray(freq), jnp.asarray(m_cos), jnp.asarray(m_one), jnp.asarray(m_lo), jnp.asarray(m_hi))


def _mod_kernel(x_ref, sc_ref, sh_ref, o_ref):
    o_ref[...] = (x_ref[...] * (1.0 + sc_ref[...]) + sh_ref[...]).astype(o_ref.dtype)


def _modulate(x2, sc, sh, batch):
    t, d = x2.shape
    s = t // batch
    bm = _tile(s, (512, 256, 128))
    nb = s // bm
    vec = pl.BlockSpec((None, 1, d), lambda b, i: (b, 0, 0))
    return pl.pallas_call(
        _mod_kernel,
        name="modulate",
        grid=(batch, nb),
        in_specs=[pl.BlockSpec((bm, d), lambda b, i: (b * nb + i, 0)), vec, vec],
        out_specs=pl.BlockSpec((bm, d), lambda b, i: (b * nb + i, 0)),
        out_shape=jax.ShapeDtypeStruct((t, d), BF16),
        compiler_params=_cparams(("arbitrary", "arbitrary"), 2 * bm * d * 6),
    )(x2, sc, sh)


def _res_ln_kernel(x_ref, y_ref, gt_ref, g_ref, b_ref, sc_ref, sh_ref, xo_ref, *ho_ref, alpha):
    z = alpha * x_ref[...] + (1.0 + gt_ref[...]) * y_ref[...].astype(F32)
    mu = jnp.mean(z, axis=-1, keepdims=True)
    zc = z - mu
    var = jnp.mean(zc * zc, axis=-1, keepdims=True)
    xn = zc * lax.rsqrt(var + 1e-5) * g_ref[...] + b_ref[...]
    xo_ref[...] = xn
    if ho_ref:
        ho_ref[0][...] = (xn * (1.0 + sc_ref[...]) + sh_ref[...]).astype(BF16)


def _res_ln(x2, y2, gt, ln_g, ln_b, sc, sh, batch, alpha, with_h):
    t, d = x2.shape
    s = t // batch
    bm = _tile(s, (256, 128))
    nb = s // bm
    vec = pl.BlockSpec((None, 1, d), lambda b, i: (b, 0, 0))
    par = pl.BlockSpec((1, d), lambda b, i: (0, 0))
    blk = pl.BlockSpec((bm, d), lambda b, i: (b * nb + i, 0))
    out_specs = [blk, blk] if with_h else [blk]
    out_shape = [jax.ShapeDtypeStruct((t, d), F32)]
    if with_h:
        out_shape.append(jax.ShapeDtypeStruct((t, d), BF16))
    outs = pl.pallas_call(
        functools.partial(_res_ln_kernel, alpha=alpha),
        name="res_ln",
        grid=(batch, nb),
        in_specs=[blk, blk, vec, par, par, vec, vec],
        out_specs=out_specs,
        out_shape=out_shape,
        compiler_params=_cparams(("arbitrary", "arbitrary"), 2 * bm * d * 14 + 6 * bm * d * 4),
    )(x2, y2, gt, ln_g.reshape(1, d), ln_b.reshape(1, d), sc, sh)
    return (outs[0], outs[1]) if with_h else (outs[0], None)


def _mm_kernel(x_ref, w_ref, o_ref):
    o_ref[...] = _dot(x_ref[...], w_ref[...]).astype(o_ref.dtype)


def _matmul(x, w, out_dtype, bm_prefs=(1024, 512, 256, 128), bn_prefs=(1024, 512, 256, 128)):
    m, k = x.shape
    n = w.shape[1]
    bm = _tile(m, bm_prefs)
    bn = _tile(n, bn_prefs)
    osz = jnp.dtype(out_dtype).itemsize
    return pl.pallas_call(
        _mm_kernel,
        name="matmul",
        grid=(m // bm, n // bn),
        in_specs=[pl.BlockSpec((bm, k), lambda i, j: (i, 0)),
                  pl.BlockSpec((k, bn), lambda i, j: (0, j))],
        out_specs=pl.BlockSpec((bm, bn), lambda i, j: (i, j)),
        out_shape=jax.ShapeDtypeStruct((m, n), out_dtype),
        compiler_params=_cparams(("arbitrary", "arbitrary"),
                                 2 * (bm * k * 2 + k * bn * 2 + bm * bn * osz) + bm * bn * 4),
    )(x, w)


def _mm_acc_kernel(x_ref, w_ref, o_ref, acc_ref):
    kk = pl.program_id(2)

    @pl.when(kk == 0)
    def _():
        acc_ref[...] = jnp.zeros_like(acc_ref)

    acc_ref[...] += _dot(x_ref[...], w_ref[...])

    @pl.when(kk == pl.num_programs(2) - 1)
    def _():
        o_ref[...] = acc_ref[...].astype(o_ref.dtype)


def _matmul_ksplit(x, w, out_dtype, nk):
    m, k = x.shape
    n = w.shape[1]
    bm = _tile(m, (1024, 512, 256, 128))
    bn = _tile(n, (512, 256, 128))
    bk = k // nk
    osz = jnp.dtype(out_dtype).itemsize
    return pl.pallas_call(
        _mm_acc_kernel,
        name="matmul_ksplit",
        grid=(m // bm, n // bn, nk),
        in_specs=[pl.BlockSpec((bm, bk), lambda i, j, q: (i, q)),
                  pl.BlockSpec((bk, bn), lambda i, j, q: (q, j))],
        out_specs=pl.BlockSpec((bm, bn), lambda i, j, q: (i, j)),
        out_shape=jax.ShapeDtypeStruct((m, n), out_dtype),
        scratch_shapes=[pltpu.VMEM((bm, bn), F32)],
        compiler_params=_cparams(("arbitrary", "arbitrary", "arbitrary"),
                                 2 * (bm * bk * 2 + bk * bn * 2 + bm * bn * osz) + 2 * bm * bn * 4),
    )(x, w)


def _proj_ac_kernel(x_ref, w_ref, cos_ref, lo_ref, hi_ref, o_ref, *, n_rot_tiles, half):
    acc = _dot(x_ref[...], w_ref[...])
    j = pl.program_id(1)

    @pl.when(j < n_rot_tiles)
    def _():
        cs, lo, hi = cos_ref[...], lo_ref[...], hi_ref[...]
        for g in range(acc.shape[1] // LANES):
            sl = slice(g * LANES, (g + 1) * LANES)
            o_ref[:, sl] = _rotate(acc[:, sl], cs, lo, hi, half).astype(o_ref.dtype)

    @pl.when(j >= n_rot_tiles)
    def _():
        o_ref[...] = acc.astype(o_ref.dtype)


def _proj_ac(h, w_ac, tabs):
    m, k = h.shape
    n = w_ac.shape[1]
    bm = _tile(m, (1024, 512, 256, 128))
    bn = _tile(2 * A_WIDTH, (768, 512, 256, 128))
    tab = pl.BlockSpec((bm, LANES), lambda i, j: (i, 0))
    return pl.pallas_call(
        functools.partial(_proj_ac_kernel, n_rot_tiles=2 * A_WIDTH // bn, half=PARTIAL_ROPE_DIM // 2),
        name="proj_ac",
        grid=(m // bm, n // bn),
        in_specs=[pl.BlockSpec((bm, k), lambda i, j: (i, 0)),
                  pl.BlockSpec((k, bn), lambda i, j: (0, j)), tab, tab, tab],
        out_specs=pl.BlockSpec((bm, bn), lambda i, j: (i, j)),
        out_shape=jax.ShapeDtypeStruct((m, n), BF16),
        compiler_params=_cparams(("arbitrary", "arbitrary"),
                                 2 * (bm * k * 2 + k * bn * 2 + bm * bn * 2 + 3 * bm * LANES * 4) + 2 * bm * bn * 4),
    )(h, w_ac, *tabs)


def _rms(x, g, eps=1e-6):
    return x * lax.rsqrt(jnp.mean(x * x, axis=-1, keepdims=True) + eps) * g


def _mla_q_kernel(h_ref, wcq_ref, g_ref, wuq_ref, cos_ref, lo_ref, hi_ref, o_ref):
    cq = _dot(h_ref[...], wcq_ref[...])
    q = _dot(_rms(cq, g_ref[...]).astype(BF16), wuq_ref[...])
    cs, lo, hi = cos_ref[...], lo_ref[...], hi_ref[...]
    for hd in range(B_HEADS):
        base = hd * B_QK_PAD
        o_ref[:, base:base + QK_NOPE] = q[:, base:base + QK_NOPE].astype(BF16)
        rope = _rotate(q[:, base + QK_NOPE:base + B_QK_PAD], cs, lo, hi, QK_ROPE // 2)
        o_ref[:, base + QK_NOPE:base + B_QK_PAD] = rope.astype(BF16)


def _mla_q(h, w_cq, g_qn, w_uq, tabs):
    m, k = h.shape
    ql = w_cq.shape[1]
    n = w_uq.shape[1]
    bm = _tile(m, (512, 256, 128))
    tab = pl.BlockSpec((bm, LANES), lambda i: (i, 0))
    return pl.pallas_call(
        _mla_q_kernel,
        name="mla_q",
        grid=(m // bm,),
        in_specs=[pl.BlockSpec((bm, k), lambda i: (i, 0)),
                  _resident((k, ql)), _resident((1, ql)), _resident((ql, n)), tab, tab, tab],
        out_specs=pl.BlockSpec((bm, n), lambda i: (i, 0)),
        out_shape=jax.ShapeDtypeStruct((m, n), BF16),
        compiler_params=_cparams(("arbitrary",),
                                 2 * (bm * k * 2 + bm * n * 2) + k * ql * 2 + ql * n * 2 + bm * (ql + n) * 8),
    )(h, w_cq, g_qn.reshape(1, ql), w_uq, *tabs)


def _mla_kv_kernel(h_ref, wc_ref, g_ref, wukv_ref, bf_ref, cos_ref, lo_ref, hi_ref, k_ref, v_ref, f_ref, *, kvl):
    ck = _dot(h_ref[...], wc_ref[...])
    kv = _dot(_rms(ck[:, :kvl], g_ref[...]).astype(BF16), wukv_ref[...])
    kr = _rotate(ck[:, kvl:kvl + LANES], cos_ref[...], lo_ref[...], hi_ref[...], QK_ROPE // 2).astype(BF16)
    for hd in range(B_HEADS):
        base = hd * B_QK_PAD
        k_ref[:, base:base + QK_NOPE] = kv[:, hd * QK_NOPE:(hd + 1) * QK_NOPE].astype(BF16)
        k_ref[:, base + QK_NOPE:base + B_QK_PAD] = kr
    v_ref[...] = kv[:, B_HEADS * QK_NOPE:].astype(BF16)
    f_ref[...] = ck[:, kvl + LANES:] + bf_ref[...]


def _mla_kv(h, w_c, g_kvn, w_ukv, b_f_row, tabs):
    m, k = h.shape
    kvl = g_kvn.shape[0]
    nc = w_c.shape[1]
    bm = _tile(m, (512, 256, 128))
    tab = pl.BlockSpec((bm, LANES), lambda i: (i, 0))
    nk = B_HEADS * B_QK_PAD
    return pl.pallas_call(
        functools.partial(_mla_kv_kernel, kvl=kvl),
        name="mla_kv",
        grid=(m // bm,),
        in_specs=[pl.BlockSpec((bm, k), lambda i: (i, 0)),
                  _resident((k, nc)), _resident((1, kvl)), _resident(w_ukv.shape), _resident((1, LANES)),
                  tab, tab, tab],
        out_specs=[pl.BlockSpec((bm, nk), lambda i: (i, 0)),
                   pl.BlockSpec((bm, B_WIDTH), lambda i: (i, 0)),
                   pl.BlockSpec((bm, LANES), lambda i: (i, 0))],
        out_shape=[jax.ShapeDtypeStruct((m, nk), BF16),
                   jax.ShapeDtypeStruct((m, B_WIDTH), BF16),
                   jax.ShapeDtypeStruct((m, LANES), F32)],
        compiler_params=_cparams(("arbitrary",),
                                 2 * (bm * k * 2 + k * nc * 2 + w_ukv.size * 2 + bm * (nk + B_WIDTH) * 2)
                                 + bm * (nc + nk + B_WIDTH) * 8),
    )(h, w_c, g_kvn.reshape(1, kvl), w_ukv, b_f_row, *tabs)


def _fox_cumsum_kernel(f_ref, o_ref):
    x = f_ref[...]
    y = (jnp.minimum(x, 0.0) - jnp.log(1.0 + jnp.exp(-jnp.abs(x)))) * LOG2E
    s = y.shape[1]
    lane = lax.broadcasted_iota(jnp.int32, y.shape, 1)
    shift = 1
    while shift < s:
        y = y + jnp.where(lane >= shift, pltpu.roll(y, shift, 1), 0.0)
        shift *= 2
    o_ref[...] = y


def _fox_cumsum(f_t):
    b, r, s = f_t.shape
    return pl.pallas_call(
        _fox_cumsum_kernel,
        name="fox_cumsum",
        grid=(b,),
        in_specs=[pl.BlockSpec((None, r, s), lambda i: (i, 0, 0))],
        out_specs=pl.BlockSpec((None, r, s), lambda i: (i, 0, 0)),
        out_shape=jax.ShapeDtypeStruct((b, r, s), F32),
        compiler_params=_cparams(("arbitrary",), 8 * r * s * 4),
    )(f_t)


def _flash_kernel(*refs, bq, bk, has_bias):
    if has_bias:
        q_ref, k_ref, v_ref, cq_ref, ck_ref, o_ref, m_sc, l_sc, acc_sc = refs
    else:
        q_ref, k_ref, v_ref, o_ref, m_sc, l_sc, acc_sc = refs
    i = pl.program_id(2)
    q = q_ref[...]
    m_sc[...] = jnp.full_like(m_sc, NEG)
    l_sc[...] = jnp.zeros_like(l_sc)
    acc_sc[...] = jnp.zeros_like(acc_sc)
    if has_bias:
        cq = cq_ref[...]

    def step(j, masked):
        start = pl.multiple_of(j * bk, bk)
        s = _dot_nt(q, k_ref[pl.ds(start, bk), :])
        if has_bias:
            s = s + cq - ck_ref[:, pl.ds(start, bk)]
        if masked:
            row = lax.broadcasted_iota(jnp.int32, (bq, bk), 0) + i * bq
            col = lax.broadcasted_iota(jnp.int32, (bq, bk), 1) + j * bk
            s = jnp.where(col <= row, s, NEG)
        m_prev = m_sc[...]
        m_new = jnp.maximum(m_prev, jnp.max(s, axis=-1, keepdims=True))
        alpha = jnp.exp2(m_prev - m_new)
        p = jnp.exp2(s - m_new)
        l_sc[...] = alpha * l_sc[...] + jnp.sum(p, axis=-1, keepdims=True)
        acc_sc[...] = alpha * acc_sc[...] + _dot(p.astype(BF16), v_ref[pl.ds(start, bk), :])
        m_sc[...] = m_new

    n_full = i * (bq // bk)

    def body(j, carry):
        step(j, False)
        return carry

    lax.fori_loop(0, n_full, body, 0)
    for t in range(bq // bk):
        step(n_full + t, True)
    o_ref[...] = (acc_sc[...] / l_sc[...]).astype(o_ref.dtype)


def _flash(q_arr, k_arr, v_arr, q_col0, k_col0, v_col0, dq, dv, heads, batch, seq, bias=None):
    bq = _tile(seq, (512, 256, 128))
    bk = bq
    nq = seq // bq
    t = batch * seq
    in_specs = [pl.BlockSpec((bq, dq), lambda b, h, i: (b * nq + i, q_col0 + h)),
                pl.BlockSpec((seq, dq), lambda b, h, i: (b, k_col0 + h)),
                pl.BlockSpec((seq, dv), lambda b, h, i: (b, v_col0 + h))]
    args = [q_arr, k_arr, v_arr]
    if bias is not None:
        c_col, c_row = bias
        in_specs += [pl.BlockSpec((None, bq, 1), lambda b, h, i: (h, b * nq + i, 0)),
                     pl.BlockSpec((None, 1, seq), lambda b, h, i: (b * F_ROWS + h, 0, 0))]
        args += [c_col, c_row]
    return pl.pallas_call(
        functools.partial(_flash_kernel, bq=bq, bk=bk, has_bias=bias is not None),
        name="flash_fox" if bias is not None else "flash_mla",
        grid=(batch, heads, nq),
        in_specs=in_specs,
        out_specs=pl.BlockSpec((bq, dv), lambda b, h, i: (b * nq + i, h)),
        out_shape=jax.ShapeDtypeStruct((t, heads * dv), BF16),
        scratch_shapes=[pltpu.VMEM((bq, 1), F32), pltpu.VMEM((bq, 1), F32), pltpu.VMEM((bq, dv), F32)],
        compiler_params=_cparams(("arbitrary", "arbitrary", "arbitrary"),
                                 2 * (seq * (dq + dv) * 2 + bq * (dq + dv) * 2 + seq * 4 + bq * 512)
                                 + 8 * bq * bk * 4 + 3 * bq * 512),
    )(*args)


def _dilated_kernel(q_ref, kp_ref, kc_ref, vp_ref, vc_ref, o_ref, lse_ref, kband, vband, *, rows, blk):
    n = pl.program_id(2)
    kband[0:blk, :] = kp_ref[...]
    kband[blk:, :] = kc_ref[...]
    vband[0:blk, :] = vp_ref[...]
    vband[blk:, :] = vc_ref[...]
    qi = lax.broadcasted_iota(jnp.int32, (blk, 2 * blk), 0)
    ki = lax.broadcasted_iota(jnp.int32, (blk, 2 * blk), 1)
    window = (ki >= qi) & (ki <= qi + blk)
    lane = lax.broadcasted_iota(jnp.int32, (blk, LANES), 1)

    def sub_block(a, carry):
        ro = pl.multiple_of(a * blk, blk)
        first_key = jnp.where(n * rows + ro > 0, 0, blk)
        mask = window & (ki >= first_key)
        lse_tile = jnp.zeros((blk, LANES), F32)
        for hd in range(A_HEADS):
            cs = slice(hd * HEAD_DIM, (hd + 1) * HEAD_DIM)
            s = _dot_nt(q_ref[pl.ds(ro, blk), cs], kband[pl.ds(ro, 2 * blk), cs])
            s = jnp.where(mask, s, NEG)
            m = jnp.max(s, axis=-1, keepdims=True)
            p = jnp.exp2(s - m)
            l = jnp.sum(p, axis=-1, keepdims=True)
            o = _dot(p.astype(BF16), vband[pl.ds(ro, 2 * blk), cs]) / l
            o_ref[pl.ds(ro, blk), cs] = o.astype(o_ref.dtype)
            lse_tile = jnp.where(lane == hd, m + jnp.log2(l), lse_tile)
        lse_ref[pl.ds(ro, blk), :] = lse_tile
        return carry

    lax.fori_loop(0, rows // blk, sub_block, 0)


def _dilated_group(qkv, row_width, batch, seq, window, dilation):
    blk = window // dilation
    sub = seq // dilation
    assert sub % blk == 0 and row_width % A_WIDTH == 0
    rows = _tile(sub, (4 * blk, 2 * blk, blk))
    nb = sub // rows
    rpb = rows // blk
    cpr = row_width // A_WIDTH
    t = batch * seq
    view = qkv.reshape(t // dilation, dilation * row_width)

    def cur(c):
        return pl.BlockSpec((rows, A_WIDTH), lambda b, r, n: (b * nb + n, cpr * r + c))

    def prev(c):
        return pl.BlockSpec((blk, A_WIDTH),
                            lambda b, r, n: (b * (sub // blk) + jnp.maximum(n * rpb - 1, 0), cpr * r + c))

    o, lse = pl.pallas_call(
        functools.partial(_dilated_kernel, rows=rows, blk=blk),
        name=f"dilated_d{dilation}",
        grid=(batch, dilation, nb),
        in_specs=[cur(0), prev(1), cur(1), prev(2), cur(2)],
        out_specs=[pl.BlockSpec((rows, A_WIDTH), lambda b, r, n: (b * nb + n, r)),
                   pl.BlockSpec((rows, LANES), lambda b, r, n: (b * nb + n, r))],
        out_shape=[jax.ShapeDtypeStruct((t // dilation, dilation * A_WIDTH), F32),
                   jax.ShapeDtypeStruct((t // dilation, dilation * LANES), F32)],
        scratch_shapes=[pltpu.VMEM((rows + blk, A_WIDTH), BF16), pltpu.VMEM((rows + blk, A_WIDTH), BF16)],
        compiler_params=_cparams(("arbitrary", "arbitrary", "arbitrary"),
                                 2 * (3 * rows + 2 * blk) * A_WIDTH * 2 + 2 * rows * (A_WIDTH + LANES) * 4
                                 + 2 * (rows + blk) * A_WIDTH * 2 + 16 * blk * 2 * blk * 4),
    )(view, view, view, view, view)
    return o.reshape(t, A_WIDTH), lse.reshape(t, LANES)


def _combine_kernel(o1, o2, o3, l1, l2, l3, out_ref):
    a1, a2, a3 = l1[...], l2[...], l3[...]
    mx = jnp.maximum(jnp.maximum(a1, a2), a3)
    e1, e2, e3 = jnp.exp2(a1 - mx), jnp.exp2(a2 - mx), jnp.exp2(a3 - mx)
    inv = 1.0 / (e1 + e2 + e3)
    w1, w2, w3 = e1 * inv, e2 * inv, e3 * inv
    for hd in range(A_HEADS):
        cs = slice(hd * HEAD_DIM, (hd + 1) * HEAD_DIM)
        acc = (w1[:, hd:hd + 1] * o1[:, cs] + w2[:, hd:hd + 1] * o2[:, cs] + w3[:, hd:hd + 1] * o3[:, cs])
        out_ref[:, cs] = acc.astype(out_ref.dtype)


def _combine(outs, lses):
    t = outs[0].shape[0]
    bm = _tile(t, (512, 256, 128))
    ob = pl.BlockSpec((bm, A_WIDTH), lambda i: (i, 0))
    lb = pl.BlockSpec((bm, LANES), lambda i: (i, 0))
    return pl.pallas_call(
        _combine_kernel,
        name="dilated_combine",
        grid=(t // bm,),
        in_specs=[ob, ob, ob, lb, lb, lb],
        out_specs=ob,
        out_shape=jax.ShapeDtypeStruct((t, A_WIDTH), BF16),
        compiler_params=_cparams(("arbitrary",), 2 * bm * (3 * A_WIDTH * 4 + 3 * LANES * 4 + A_WIDTH * 2) + 8 * bm * LANES * 4),
    )(*outs, *lses)


def _merge_kernel(h_ref, oa_ref, ob_ref, oc_ref, wg0, wg1, wg2, wa, wb, wc, o_ref):
    h = h_ref[...]
    acc = jax.nn.sigmoid(_dot(h, wg0[...])) * _dot(oa_ref[...], wa[...])
    acc += jax.nn.sigmoid(_dot(h, wg1[...])) * _dot(ob_ref[...], wb[...])
    acc += jax.nn.sigmoid(_dot(h, wg2[...])) * _dot(oc_ref[...], wc[...])
    o_ref[...] = acc.astype(o_ref.dtype)


def _merge(h, oa, ob, oc, w_gate, w_a, w_b, w_c):
    m, d = h.shape
    bm = _tile(m, (512, 256, 128))
    bn = _tile(d, (512, 256, 128))
    nj = d // bn

    def rows(width):
        return pl.BlockSpec((bm, width), lambda i, j: (i, 0))

    def gate(g):
        return pl.BlockSpec((d, bn), lambda i, j: (0, g * nj + j))

    def branch(width):
        return pl.BlockSpec((width, bn), lambda i, j: (0, j))

    k_all = 3 * d + A_WIDTH + B_WIDTH + C_WIDTH
    return pl.pallas_call(
        _merge_kernel,
        name="gate_merge",
        grid=(m // bm, nj),
        in_specs=[rows(d), rows(A_WIDTH), rows(B_WIDTH), rows(C_WIDTH), gate(0), gate(1), gate(2),
                  branch(A_WIDTH), branch(B_WIDTH), branch(C_WIDTH)],
        out_specs=pl.BlockSpec((bm, bn), lambda i, j: (i, j)),
        out_shape=jax.ShapeDtypeStruct((m, d), BF16),
        compiler_params=_cparams(("arbitrary", "arbitrary"),
                                 2 * (bm * (d + A_WIDTH + B_WIDTH + C_WIDTH) * 2 + k_all * bn * 2 + bm * bn * 2)
                                 + 8 * bm * bn * 4),
    )(h, oa, ob, oc, w_gate, w_gate, w_gate, w_a, w_b, w_c)


def _ffn_in_kernel(h_ref, wa_ref, wb_ref, o_ref):
    h = h_ref[...]
    a = _dot(h, wa_ref[...])
    b = _dot(h, wb_ref[...])
    o_ref[...] = (a * jax.nn.sigmoid(a) * b).astype(o_ref.dtype)


def _ffn_in(h, w_in, d_ff):
    m, d = h.shape
    bm = _tile(m, (1024, 512, 256, 128))
    bn = _tile(d_ff, (512, 256, 128))
    nj = d_ff // bn
    return pl.pallas_call(
        _ffn_in_kernel,
        name="ffn_in",
        grid=(m // bm, nj),
        in_specs=[pl.BlockSpec((bm, d), lambda i, j: (i, 0)),
                  pl.BlockSpec((d, bn), lambda i, j: (0, j)),
                  pl.BlockSpec((d, bn), lambda i, j: (0, nj + j))],
        out_specs=pl.BlockSpec((bm, bn), lambda i, j: (i, j)),
        out_shape=jax.ShapeDtypeStruct((m, d_ff), BF16),
        compiler_params=_cparams(("arbitrary", "arbitrary"),
                                 2 * (bm * d * 2 + 2 * d * bn * 2 + bm * bn * 2) + 4 * bm * bn * 4),
    )(h, w_in, w_in)


def _layer(x2, h, ada_l, ada_next, batch, seq, tabs_a, tabs_b, w, alpha, last):
    d = x2.shape[1]
    t = batch * seq
    sh1, sc1, gt1, sh2, sc2, gt2 = [a.reshape(batch, 1, d) for a in jnp.split(ada_l, 6, axis=-1)]
    del sh1, sc1

    qkv_ac = _proj_ac(h, w["ac"], tabs_a)
    q_b = _mla_q(h, w["cq"], w["g_qn"], w["uq"], tabs_b)
    k_b, v_b, f_logit = _mla_kv(h, w["ckv"], w["g_kvn"], w["ukv"], w["b_f"], tabs_b)

    outs, lses = [], []
    for window, dilation in A_PATTERNS:
        o, l = _dilated_group(qkv_ac, qkv_ac.shape[1], batch, seq, window, dilation)
        outs.append(o)
        lses.append(l)
    o_a = _combine(outs, lses)

    o_b = _flash(q_b, k_b, v_b, 0, 0, 0, B_QK_PAD, V_DIM, B_HEADS, batch, seq)

    f_t = f_logit[:, :F_ROWS].reshape(batch, seq, F_ROWS).transpose(0, 2, 1)
    c_t = _fox_cumsum(f_t)
    c_col = c_t.transpose(1, 0, 2).reshape(F_ROWS, t, 1)
    c_row = c_t.reshape(batch * F_ROWS, 1, seq)
    nh = A_WIDTH // HEAD_DIM
    o_c = _flash(qkv_ac, qkv_ac, qkv_ac, 3 * nh, 4 * nh, 5 * nh, HEAD_DIM, HEAD_DIM, C_HEADS, batch, seq,
                 bias=(c_col, c_row))

    merged = _merge(h, o_a, o_b, o_c, w["gate"], w["br_a"], w["br_b"], w["br_c"])
    y = _matmul(merged, w["o"], F32)
    x2, h2 = _res_ln(x2, y, gt1, w["ln1_g"], w["ln1_b"], sc2, sh2, batch, alpha, True)

    act = _ffn_in(h2, w["ffn_in"], w["d_ff"])
    nk = 2 if (w["d_ff"] // 2) % LANES == 0 else 1
    y = _matmul_ksplit(act, w["ffn_out"], F32, nk)
    if last:
        x2, hn = _res_ln(x2, y, gt2, w["ln2_g"], w["ln2_b"], sc2, sh2, batch, alpha, False)
    else:
        sh1n, sc1n = [a.reshape(batch, 1, d) for a in jnp.split(ada_next, 6, axis=-1)[:2]]
        x2, hn = _res_ln(x2, y, gt2, w["ln2_g"], w["ln2_b"], sc1n, sh1n, batch, alpha, True)
    return x2, hn


def _prep_weights(l, w_in, b_f, g_qn, w_uq, g_kvn, w_ukv, w_br_a, w_br_b, w_br_c, w_o,
                  ln1_g, ln1_b, w_ffn_in, w_ffn_out, ln2_g, ln2_b):
    d = w_in.shape[1]
    ql = g_qn.shape[1]
    kvl = g_kvn.shape[1]
    wi = w_in[l]
    widths = (A_WIDTH, A_WIDTH, A_WIDTH, ql, kvl, QK_ROPE, C_WIDTH, C_WIDTH, C_WIDTH, C_HEADS, N_BRANCH * d)
    offs = np.concatenate([[0], np.cumsum(widths)])
    qa, ka, va, cq, ckv, kr, qc, kc, vc, fl, gate = [wi[:, offs[i]:offs[i + 1]] for i in range(len(widths))]
    scale_a = HEAD_DIM ** -0.5 * LOG2E
    scale_b = (QK_NOPE + QK_ROPE) ** -0.5 * LOG2E
    w_ac = jnp.concatenate([qa * scale_a, ka, va, qc * scale_a, kc, vc], axis=1).astype(BF16)
    zpad = lambda n: jnp.zeros((d, n), F32)
    w_ckv = jnp.concatenate([ckv, kr, zpad(LANES - QK_ROPE), fl, zpad(LANES - C_HEADS)], axis=1).astype(BF16)
    uq = w_uq[l].reshape(ql, B_HEADS, QK_NOPE + QK_ROPE) * scale_b
    uq = jnp.pad(uq, ((0, 0), (0, 0), (0, B_QK_PAD - QK_NOPE - QK_ROPE))).reshape(ql, B_HEADS * B_QK_PAD)
    ukv = w_ukv[l].reshape(kvl, B_HEADS, 2, QK_NOPE).transpose(0, 2, 1, 3).reshape(kvl, 2 * B_HEADS * QK_NOPE)
    b_f_row = jnp.pad(b_f[l], (0, LANES - C_HEADS)).reshape(1, LANES)
    return dict(ac=w_ac, cq=cq.astype(BF16), g_qn=g_qn[l], uq=uq.astype(BF16), ckv=w_ckv, g_kvn=g_kvn[l],
                ukv=ukv.astype(BF16), b_f=b_f_row, gate=gate.astype(BF16), br_a=w_br_a[l].astype(BF16),
                br_b=w_br_b[l].astype(BF16), br_c=w_br_c[l].astype(BF16), o=w_o[l].astype(BF16),
                ln1_g=ln1_g[l], ln1_b=ln1_b[l], ffn_in=w_ffn_in[l].astype(BF16),
                ffn_out=w_ffn_out[l].astype(BF16), d_ff=w_ffn_out.shape[1], ln2_g=ln2_g[l], ln2_b=ln2_b[l])


def kernel(x, c, positions, w_ada, b_ada, w_in, b_f, g_qn, w_uq, g_kvn, w_ukv, w_br_a, w_br_b, w_br_c, w_o,
           ln1_g, ln1_b, w_ffn_in, w_ffn_out, ln2_g, ln2_b):
    batch, seq, d = x.shape
    depth = w_ada.shape[0]
    alpha = (2.0 * depth) ** 0.25
    t = batch * seq
    x2 = x.reshape(t, d)
    ada = _ada(c, w_ada, b_ada)
    pos_col = positions.astype(F32).reshape(t, 1)
    tabs_a = _rope_tables(pos_col, PARTIAL_ROPE_DIM)
    tabs_b = _rope_tables(pos_col, QK_ROPE)
    sh1, sc1 = [a.reshape(batch, 1, d) for a in jnp.split(ada[0], 6, axis=-1)[:2]]
    h = _modulate(x2, sc1, sh1, batch)
    for l in range(depth):
        w = _prep_weights(l, w_in, b_f, g_qn, w_uq, g_kvn, w_ukv, w_br_a, w_br_b, w_br_c, w_o,
                          ln1_g, ln1_b, w_ffn_in, w_ffn_out, ln2_g, ln2_b)
        last = l == depth - 1
        x2, h = _layer(x2, h, ada[l], None if last else ada[l + 1], batch, seq, tabs_a, tabs_b, w, alpha, last)
    return x2.reshape(batch, seq, d)
```

```python
import functools
import math

import jax
import jax.numpy as jnp
import numpy as np
from jax import lax
from jax.experimental import pallas as pl
from jax.experimental.pallas import tpu as pltpu

HEAD_DIM = 128
ROPE_THETA = 500000.0
PARTIAL_ROPE_DIM = HEAD_DIM // 4
A_HEADS = 12
A_PATTERNS = ((128, 1), (512, 4), (2048, 16))
B_HEADS = 8
QK_NOPE = 128
QK_ROPE = 64
V_DIM = 128
C_HEADS = 12
N_BRANCH = 3
A_WIDTH = A_HEADS * HEAD_DIM
B_WIDTH = B_HEADS * V_DIM
C_WIDTH = C_HEADS * HEAD_DIM
B_QK_PAD = 256
F_ROWS = 16
NEG = -1e30
LOG2E = math.log2(math.e)
LANES = 128
V7X_VMEM_CAP = 60 * 1024 * 1024

BF16 = jnp.bfloat16
F32 = jnp.float32


def _cparams(semantics, vmem_estimate):
    limit = int(min(max(vmem_estimate * 5 // 4, 32 * 1024 * 1024), V7X_VMEM_CAP))
    return pltpu.CompilerParams(dimension_semantics=semantics, vmem_limit_bytes=limit)


def _tile(n, prefs):
    for p in prefs:
        if n % p == 0:
            return p
    return n


def _resident(shape):
    return pl.BlockSpec(shape, lambda i: (0,) * len(shape), pipeline_mode=pl.Buffered(1))


def _dot(a, b):
    return jnp.dot(a, b, preferred_element_type=F32)


def _dot_nt(a, b):
    return lax.dot_general(a, b, (((1,), (1,)), ((), ())), preferred_element_type=F32)


def _rotate(t, cos, sin_lo, sin_hi, half):
    return t * cos + pltpu.roll(t, half, 1) * sin_hi + pltpu.roll(t, LANES - half, 1) * sin_lo


def _ada_kernel(c_ref, w_ref, b_ref, o_ref, s_sc):
    @pl.when((pl.program_id(0) == 0) & (pl.program_id(1) == 0))
    def _():
        cv = c_ref[...]
        s_sc[...] = cv * jax.nn.sigmoid(cv)

    for b in range(s_sc.shape[0]):
        s = s_sc[b]
        cols = [jnp.sum(w_ref[:, g * LANES:(g + 1) * LANES] * s, axis=0, keepdims=True)
                for g in range(w_ref.shape[1] // LANES)]
        o_ref[b:b + 1, :] = jnp.concatenate(cols, axis=1) + b_ref[...]


def _ada(c, w_ada, b_ada):
    depth, d, n = w_ada.shape
    b = c.shape[0]
    bn = _tile(n, (1024, 512, 256, 128))
    c_rep = jnp.broadcast_to(c[:, :, None], (b, d, LANES))
    return pl.pallas_call(
        _ada_kernel,
        name="ada",
        grid=(depth, n // bn),
        in_specs=[pl.BlockSpec((b, d, LANES), lambda l, j: (0, 0, 0)),
                  pl.BlockSpec((None, d, bn), lambda l, j: (l, 0, j)),
                  pl.BlockSpec((None, 1, bn), lambda l, j: (l, 0, j))],
        out_specs=pl.BlockSpec((None, b, bn), lambda l, j: (l, 0, j)),
        out_shape=jax.ShapeDtypeStruct((depth, b, n), F32),
        scratch_shapes=[pltpu.VMEM((b, d, LANES), F32)],
        compiler_params=_cparams(("arbitrary", "arbitrary"), 2 * d * bn * 4 + 3 * b * d * LANES * 4),
    )(c_rep, w_ada, b_ada.reshape(depth, 1, n))


def _rope_table_kernel(pos_ref, freq_ref, mc_ref, m1_ref, mlo_ref, mhi_ref, cos_ref, lo_ref, hi_ref):
    ang = pos_ref[...] * freq_ref[...]
    cs = jnp.cos(ang)
    sn = jnp.sin(ang)
    cos_ref[...] = cs * mc_ref[...] + m1_ref[...]
    lo_ref[...] = sn * mlo_ref[...]
    hi_ref[...] = sn * mhi_ref[...]


def _rope_tables(pos_col, rot_dim):
    t = pos_col.shape[0]
    half = rot_dim // 2
    inv = np.exp(-math.log(ROPE_THETA) * np.arange(half, dtype=np.float32) * np.float32(2.0 / rot_dim))
    lane = np.arange(LANES)
    freq = np.where(lane < rot_dim, inv[lane % half], 0.0).astype(np.float32)[None]
    m_cos = (lane < rot_dim).astype(np.float32)[None]
    m_one = (lane >= rot_dim).astype(np.float32)[None]
    m_lo = np.where(lane < half, -1.0, 0.0).astype(np.float32)[None]
    m_hi = np.where((lane >= half) & (lane < rot_dim), 1.0, 0.0).astype(np.float32)[None]
    bm = _tile(t, (2048, 1024, 512, 256, 128))
    row = pl.BlockSpec((1, LANES), lambda i: (0, 0))
    tab = pl.BlockSpec((bm, LANES), lambda i: (i, 0))
    shp = jax.ShapeDtypeStruct((t, LANES), F32)
    return pl.pallas_call(
        _rope_table_kernel,
        name="rope_tables",
        grid=(t // bm,),
        in_specs=[pl.BlockSpec((bm, 1), lambda i: (i, 0)), row, row, row, row, row],
        out_specs=[tab, tab, tab],
        out_shape=[shp, shp, shp],
        compiler_params=_cparams(("arbitrary",), 16 * bm * LANES * 4),
    )(pos_col, jnp.asarray(freq), jnp.asarray(m_cos), jnp.asarray(m_one), jnp.asarray(m_lo), jnp.asarray(m_hi))


def _mod_kernel(x_ref, sc_ref, sh_ref, o_ref):
    o_ref[...] = (x_ref[...] * (1.0 + sc_ref[...]) + sh_ref[...]).astype(o_ref.dtype)


def _modulate(x2, sc, sh, batch):
    t, d = x2.shape
    s = t // batch
    bm = _tile(s, (512, 256, 128))
    nb = s // bm
    vec = pl.BlockSpec((None, 1, d), lambda b, i: (b, 0, 0))
    return pl.pallas_call(
        _mod_kernel,
        name="modulate",
        grid=(batch, nb),
        in_specs=[pl.BlockSpec((bm, d), lambda b, i: (b * nb + i, 0)), vec, vec],
        out_specs=pl.BlockSpec((bm, d), lambda b, i: (b * nb + i, 0)),
        out_shape=jax.ShapeDtypeStruct((t, d), BF16),
        compiler_params=_cparams(("arbitrary", "arbitrary"), 2 * bm * d * 6),
    )(x2, sc, sh)


def _res_ln_kernel(x_ref, y_ref, gt_ref, g_ref, b_ref, sc_ref, sh_ref, xo_ref, *ho_ref, alpha):
    z = alpha * x_ref[...] + (1.0 + gt_ref[...]) * y_ref[...].astype(F32)
    mu = jnp.mean(z, axis=-1, keepdims=True)
    zc = z - mu
    var = jnp.mean(zc * zc, axis=-1, keepdims=True)
    xn = zc * lax.rsqrt(var + 1e-5) * g_ref[...] + b_ref[...]
    xo_ref[...] = xn
    if ho_ref:
        ho_ref[0][...] = (xn * (1.0 + sc_ref[...]) + sh_ref[...]).astype(BF16)


def _res_ln(x2, y2, gt, ln_g, ln_b, sc, sh, batch, alpha, with_h):
    t, d = x2.shape
    s = t // batch
    bm = _tile(s, (256, 128))
    nb = s // bm
    vec = pl.BlockSpec((None, 1, d), lambda b, i: (b, 0, 0))
    par = pl.BlockSpec((1, d), lambda b, i: (0, 0))
    blk = pl.BlockSpec((bm, d), lambda b, i: (b * nb + i, 0))
    out_specs = [blk, blk] if with_h else [blk]
    out_shape = [jax.ShapeDtypeStruct((t, d), F32)]
    if with_h:
        out_shape.append(jax.ShapeDtypeStruct((t, d), BF16))
    outs = pl.pallas_call(
        functools.partial(_res_ln_kernel, alpha=alpha),
        name="res_ln",
        grid=(batch, nb),
        in_specs=[blk, blk, vec, par, par, vec, vec],
        out_specs=out_specs,
        out_shape=out_shape,
        compiler_params=_cparams(("arbitrary", "arbitrary"), 2 * bm * d * 14 + 6 * bm * d * 4),
    )(x2, y2, gt, ln_g.reshape(1, d), ln_b.reshape(1, d), sc, sh)
    return (outs[0], outs[1]) if with_h else (outs[0], None)


def _mm_kernel(x_ref, w_ref, o_ref):
    o_ref[...] = _dot(x_ref[...], w_ref[...]).astype(o_ref.dtype)


def _matmul(x, w, out_dtype, bm_prefs=(1024, 512, 256, 128), bn_prefs=(1024, 512, 256, 128)):
    m, k = x.shape
    n = w.shape[1]
    bm = _tile(m, bm_prefs)
    bn = _tile(n, bn_prefs)
    osz = jnp.dtype(out_dtype).itemsize
    return pl.pallas_call(
        _mm_kernel,
        name="matmul",
        grid=(m // bm, n // bn),
        in_specs=[pl.BlockSpec((bm, k), lambda i, j: (i, 0)),
                  pl.BlockSpec((k, bn), lambda i, j: (0, j))],
        out_specs=pl.BlockSpec((bm, bn), lambda i, j: (i, j)),
        out_shape=jax.ShapeDtypeStruct((m, n), out_dtype),
        compiler_params=_cparams(("arbitrary", "arbitrary"),
                                 2 * (bm * k * 2 + k * bn * 2 + bm * bn * osz) + bm * bn * 4),
    )(x, w)


def _mm_acc_kernel(x_ref, w_ref, o_ref, acc_ref):
    kk = pl.program_id(2)

    @pl.when(kk == 0)
    def _():
        acc_ref[...] = jnp.zeros_like(acc_ref)

    acc_ref[...] += _dot(x_ref[...], w_ref[...])

    @pl.when(kk == pl.num_programs(2) - 1)
    def _():
        o_ref[...] = acc_ref[...].astype(o_ref.dtype)


def _matmul_ksplit(x, w, out_dtype, nk):
    m, k = x.shape
    n = w.shape[1]
    bm = _tile(m, (1024, 512, 256, 128))
    bn = _tile(n, (512, 256, 128))
    bk = k // nk
    osz = jnp.dtype(out_dtype).itemsize
    return pl.pallas_call(
        _mm_acc_kernel,
        name="matmul_ksplit",
        grid=(m // bm, n // bn, nk),
        in_specs=[pl.BlockSpec((bm, bk), lambda i, j, q: (i, q)),
                  pl.BlockSpec((bk, bn), lambda i, j, q: (q, j))],
        out_specs=pl.BlockSpec((bm, bn), lambda i, j, q: (i, j)),
        out_shape=jax.ShapeDtypeStruct((m, n), out_dtype),
        scratch_shapes=[pltpu.VMEM((bm, bn), F32)],
        compiler_params=_cparams(("arbitrary", "arbitrary", "arbitrary"),
                                 2 * (bm * bk * 2 + bk * bn * 2 + bm * bn * osz) + 2 * bm * bn * 4),
    )(x, w)


def _proj_ac_kernel(x_ref, w_ref, cos_ref, lo_ref, hi_ref, o_ref, *, n_rot_tiles, half):
    acc = _dot(x_ref[...], w_ref[...])
    j = pl.program_id(1)

    @pl.when(j < n_rot_tiles)
    def _():
        cs, lo, hi = cos_ref[...], lo_ref[...], hi_ref[...]
        for g in range(acc.shape[1] // LANES):
            sl = slice(g * LANES, (g + 1) * LANES)
            o_ref[:, sl] = _rotate(acc[:, sl], cs, lo, hi, half).astype(o_ref.dtype)

    @pl.when(j >= n_rot_tiles)
    def _():
        o_ref[...] = acc.astype(o_ref.dtype)


def _proj_ac(h, w_ac, tabs):
    m, k = h.shape
    n = w_ac.shape[1]
    bm = _tile(m, (1024, 512, 256, 128))
    bn = _tile(2 * A_WIDTH, (768, 512, 256, 128))
    tab = pl.BlockSpec((bm, LANES), lambda i, j: (i, 0))
    return pl.pallas_call(
        functools.partial(_proj_ac_kernel, n_rot_tiles=2 * A_WIDTH // bn, half=PARTIAL_ROPE_DIM // 2),
        name="proj_ac",
        grid=(m // bm, n // bn),
        in_specs=[pl.BlockSpec((bm, k), lambda i, j: (i, 0)),
                  pl.BlockSpec((k, bn), lambda i, j: (0, j)), tab, tab, tab],
        out_specs=pl.BlockSpec((bm, bn), lambda i, j: (i, j)),
        out_shape=jax.ShapeDtypeStruct((m, n), BF16),
        compiler_params=_cparams(("arbitrary", "arbitrary"),
                                 2 * (bm * k * 2 + k * bn * 2 + bm * bn * 2 + 3 * bm * LANES * 4) + 2 * bm * bn * 4),
    )(h, w_ac, *tabs)


def _rms(x, g, eps=1e-6):
    return x * lax.rsqrt(jnp.mean(x * x, axis=-1, keepdims=True) + eps) * g


def _mla_q_kernel(h_ref, wcq_ref, g_ref, wuq_ref, cos_ref, lo_ref, hi_ref, o_ref):
    cq = _dot(h_ref[...], wcq_ref[...])
    q = _dot(_rms(cq, g_ref[...]).astype(BF16), wuq_ref[...])
    cs, lo, hi = cos_ref[...], lo_ref[...], hi_ref[...]
    for hd in range(B_HEADS):
        base = hd * B_QK_PAD
        o_ref[:, base:base + QK_NOPE] = q[:, base:base + QK_NOPE].astype(BF16)
        rope = _rotate(q[:, base + QK_NOPE:base + B_QK_PAD], cs, lo, hi, QK_ROPE // 2)
        o_ref[:, base + QK_NOPE:base + B_QK_PAD] = rope.astype(BF16)


def _mla_q(h, w_cq, g_qn, w_uq, tabs):
    m, k = h.shape
    ql = w_cq.shape[1]
    n = w_uq.shape[1]
    bm = _tile(m, (512, 256, 128))
    tab = pl.BlockSpec((bm, LANES), lambda i: (i, 0))
    return pl.pallas_call(
        _mla_q_kernel,
        name="mla_q",
        grid=(m // bm,),
        in_specs=[pl.BlockSpec((bm, k), lambda i: (i, 0)),
                  _resident((k, ql)), _resident((1, ql)), _resident((ql, n)), tab, tab, tab],
        out_specs=pl.BlockSpec((bm, n), lambda i: (i, 0)),
        out_shape=jax.ShapeDtypeStruct((m, n), BF16),
        compiler_params=_cparams(("arbitrary",),
                                 2 * (bm * k * 2 + bm * n * 2) + k * ql * 2 + ql * n * 2 + bm * (ql + n) * 8),
    )(h, w_cq, g_qn.reshape(1, ql), w_uq, *tabs)


def _mla_kv_kernel(h_ref, wc_ref, g_ref, wukv_ref, bf_ref, cos_ref, lo_ref, hi_ref, k_ref, v_ref, f_ref, *, kvl):
    ck = _dot(h_ref[...], wc_ref[...])
    kv = _dot(_rms(ck[:, :kvl], g_ref[...]).astype(BF16), wukv_ref[...])
    kr = _rotate(ck[:, kvl:kvl + LANES], cos_ref[...], lo_ref[...], hi_ref[...], QK_ROPE // 2).astype(BF16)
    for hd in range(B_HEADS):
        base = hd * B_QK_PAD
        k_ref[:, base:base + QK_NOPE] = kv[:, hd * QK_NOPE:(hd + 1) * QK_NOPE].astype(BF16)
        k_ref[:, base + QK_NOPE:base + B_QK_PAD] = kr
    v_ref[...] = kv[:, B_HEADS * QK_NOPE:].astype(BF16)
    f_ref[...] = ck[:, kvl + LANES:] + bf_ref[...]


def _mla_kv(h, w_c, g_kvn, w_ukv, b_f_row, tabs):
    m, k = h.shape
    kvl = g_kvn.shape[0]
    nc = w_c.shape[1]
    bm = _tile(m, (512, 256, 128))
    tab = pl.BlockSpec((bm, LANES), lambda i: (i, 0))
    nk = B_HEADS * B_QK_PAD
    return pl.pallas_call(
        functools.partial(_mla_kv_kernel, kvl=kvl),
        name="mla_kv",
        grid=(m // bm,),
        in_specs=[pl.BlockSpec((bm, k), lambda i: (i, 0)),
                  _resident((k, nc)), _resident((1, kvl)), _resident(w_ukv.shape), _resident((1, LANES)),
                  tab, tab, tab],
        out_specs=[pl.BlockSpec((bm, nk), lambda i: (i, 0)),
                   pl.BlockSpec((bm, B_WIDTH), lambda i: (i, 0)),
                   pl.BlockSpec((bm, LANES), lambda i: (i, 0))],
        out_shape=[jax.ShapeDtypeStruct((m, nk), BF16),
                   jax.ShapeDtypeStruct((m, B_WIDTH), BF16),
                   jax.ShapeDtypeStruct((m, LANES), F32)],
        compiler_params=_cparams(("arbitrary",),
                                 2 * (bm * k * 2 + k * nc * 2 + w_ukv.size * 2 + bm * (nk + B_WIDTH) * 2)
                                 + bm * (nc + nk + B_WIDTH) * 8),
    )(h, w_c, g_kvn.reshape(1, kvl), w_ukv, b_f_row, *tabs)


def _fox_cumsum_kernel(f_ref, o_ref):
    x = f_ref[...]
    y = (jnp.minimum(x, 0.0) - jnp.log(1.0 + jnp.exp(-jnp.abs(x)))) * LOG2E
    s = y.shape[1]
    lane = lax.broadcasted_iota(jnp.int32, y.shape, 1)
    shift = 1
    while shift < s:
        y = y + jnp.where(lane >= shift, pltpu.roll(y, shift, 1), 0.0)
        shift *= 2
    o_ref[...] = y


def _fox_cumsum(f_t):
    b, r, s = f_t.shape
    return pl.pallas_call(
        _fox_cumsum_kernel,
        name="fox_cumsum",
        grid=(b,),
        in_specs=[pl.BlockSpec((None, r, s), lambda i: (i, 0, 0))],
        out_specs=pl.BlockSpec((None, r, s), lambda i: (i, 0, 0)),
        out_shape=jax.ShapeDtypeStruct((b, r, s), F32),
        compiler_params=_cparams(("arbitrary",), 8 * r * s * 4),
    )(f_t)


def _flash_kernel(*refs, bq, bk, has_bias):
    if has_bias:
        q_ref, k_ref, v_ref, cq_ref, ck_ref, o_ref, acc_sc, s_sc, p_sc = refs
    else:
        q_ref, k_ref, v_ref, o_ref, acc_sc, s_sc, p_sc = refs
    assert bq == bk
    i = pl.program_id(2)

    def scores(j):
        start = pl.multiple_of(j * bk, bk)
        s = _dot_nt(k_ref[pl.ds(start, bk), :], q_ref[...])
        if has_bias:
            ck = ck_ref[pl.ds(start, bk), :]
            s = s + cq_ref[...] - jnp.concatenate([ck] * (bq // LANES), axis=1)
        return s

    def accumulate(j, alpha, slot):
        start = pl.multiple_of(j * bk, bk)
        pv = lax.dot_general(v_ref[pl.ds(start, bk), :], p_sc[slot], (((0,), (0,)), ((), ())),
                             preferred_element_type=F32)
        acc_sc[...] = alpha * acc_sc[...] + pv

    def softmax_update(m_prev, l_prev, masked, slot):
        s = s_sc[slot]
        if masked:
            key = lax.broadcasted_iota(jnp.int32, (bk, bq), 0)
            qry = lax.broadcasted_iota(jnp.int32, (bk, bq), 1)
            s = jnp.where(key <= qry, s, NEG)
        m_new = jnp.maximum(m_prev, jnp.max(s, axis=0, keepdims=True))
        alpha = jnp.exp2(m_prev - m_new)
        p = jnp.exp2(s - m_new)
        l_new = alpha * l_prev + jnp.sum(p, axis=0, keepdims=True)
        return m_new, l_new, alpha, p.astype(BF16)

    acc_sc[...] = jnp.zeros_like(acc_sc)
    p_sc[1] = jnp.zeros((bk, bq), BF16)
    s_sc[0] = scores(0)

    def body(j, carry):
        m, l, alpha_prev = carry
        cur = j % 2
        accumulate(jnp.maximum(j - 1, 0), alpha_prev, 1 - cur)
        m, l, alpha, p = softmax_update(m, l, False, cur)
        p_sc[cur] = p
        s_sc[1 - cur] = scores(j + 1)
        return m, l, alpha

    init = (jnp.full((1, bq), NEG, F32), jnp.zeros((1, bq), F32), jnp.ones((1, bq), F32))
    m, l, alpha_prev = lax.fori_loop(0, i, body, init)
    cur = i % 2
    accumulate(jnp.maximum(i - 1, 0), alpha_prev, 1 - cur)
    m, l, alpha, p = softmax_update(m, l, True, cur)
    p_sc[cur] = p
    accumulate(i, alpha, cur)
    o_ref[...] = jnp.transpose(acc_sc[...] / l).astype(o_ref.dtype)


def _flash(q_arr, k_arr, v_arr, q_col0, k_col0, v_col0, dq, dv, heads, batch, seq, bias=None):
    bq = _tile(seq, (512, 256, 128))
    bk = bq
    nq = seq // bq
    t = batch * seq
    in_specs = [pl.BlockSpec((bq, dq), lambda b, h, i: (b * nq + i, q_col0 + h)),
                pl.BlockSpec((seq, dq), lambda b, h, i: (b, k_col0 + h)),
                pl.BlockSpec((seq, dv), lambda b, h, i: (b, v_col0 + h))]
    args = [q_arr, k_arr, v_arr]
    if bias is not None:
        c_row, c_rep = bias
        in_specs += [pl.BlockSpec((None, 1, bq), lambda b, h, i: (b * F_ROWS + h, 0, i)),
                     pl.BlockSpec((None, seq, LANES), lambda b, h, i: (b * heads + h, 0, 0))]
        args += [c_row, c_rep]
    return pl.pallas_call(
        functools.partial(_flash_kernel, bq=bq, bk=bk, has_bias=bias is not None),
        name="flash_fox" if bias is not None else "flash_mla",
        grid=(batch, heads, nq),
        in_specs=in_specs,
        out_specs=pl.BlockSpec((bq, dv), lambda b, h, i: (b * nq + i, h)),
        out_shape=jax.ShapeDtypeStruct((t, heads * dv), BF16),
        scratch_shapes=[pltpu.VMEM((dv, bq), F32), pltpu.VMEM((2, bk, bq), F32), pltpu.VMEM((2, bk, bq), BF16)],
        compiler_params=_cparams(("arbitrary", "arbitrary", "arbitrary"),
                                 2 * (seq * (dq + dv) * 2 + bq * (dq + dv) * 2 + seq * LANES * 4)
                                 + 8 * bq * bk * 4),
    )(*args)


def _dilated_kernel(q_ref, kp_ref, kc_ref, vp_ref, vc_ref, o_ref, lse_ref, kband, vband, *, rows, blk):
    n = pl.program_id(2)
    kband[0:blk, :] = kp_ref[...]
    kband[blk:, :] = kc_ref[...]
    vband[0:blk, :] = vp_ref[...]
    vband[blk:, :] = vc_ref[...]
    qi = lax.broadcasted_iota(jnp.int32, (blk, 2 * blk), 0)
    ki = lax.broadcasted_iota(jnp.int32, (blk, 2 * blk), 1)
    window = (ki >= qi) & (ki <= qi + blk)
    lane = lax.broadcasted_iota(jnp.int32, (blk, LANES), 1)

    def sub_block(a, carry):
        ro = pl.multiple_of(a * blk, blk)
        first_key = jnp.where(n * rows + ro > 0, 0, blk)
        mask = window & (ki >= first_key)
        lse_tile = jnp.zeros((blk, LANES), F32)
        for hd in range(A_HEADS):
            cs = slice(hd * HEAD_DIM, (hd + 1) * HEAD_DIM)
            s = _dot_nt(q_ref[pl.ds(ro, blk), cs], kband[pl.ds(ro, 2 * blk), cs])
            s = jnp.where(mask, s, NEG)
            m = jnp.max(s, axis=-1, keepdims=True)
            p = jnp.exp2(s - m)
            l = jnp.sum(p, axis=-1, keepdims=True)
            o = _dot(p.astype(BF16), vband[pl.ds(ro, 2 * blk), cs]) / l
            o_ref[pl.ds(ro, blk), cs] = o.astype(o_ref.dtype)
            lse_tile = jnp.where(lane == hd, m + jnp.log2(l), lse_tile)
        lse_ref[pl.ds(ro, blk), :] = lse_tile
        return carry

    lax.fori_loop(0, rows // blk, sub_block, 0)


def _dilated_group(qkv, row_width, batch, seq, window, dilation):
    blk = window // dilation
    sub = seq // dilation
    assert sub % blk == 0 and row_width % A_WIDTH == 0
    rows = _tile(sub, (4 * blk, 2 * blk, blk))
    nb = sub // rows
    rpb = rows // blk
    cpr = row_width // A_WIDTH
    t = batch * seq
    view = qkv.reshape(t // dilation, dilation * row_width)

    def cur(c):
        return pl.BlockSpec((rows, A_WIDTH), lambda b, r, n: (b * nb + n, cpr * r + c))

    def prev(c):
        return pl.BlockSpec((blk, A_WIDTH),
                            lambda b, r, n: (b * (sub // blk) + jnp.maximum(n * rpb - 1, 0), cpr * r + c))

    o, lse = pl.pallas_call(
        functools.partial(_dilated_kernel, rows=rows, blk=blk),
        name=f"dilated_d{dilation}",
        grid=(batch, dilation, nb),
        in_specs=[cur(0), prev(1), cur(1), prev(2), cur(2)],
        out_specs=[pl.BlockSpec((rows, A_WIDTH), lambda b, r, n: (b * nb + n, r)),
                   pl.BlockSpec((rows, LANES), lambda b, r, n: (b * nb + n, r))],
        out_shape=[jax.ShapeDtypeStruct((t // dilation, dilation * A_WIDTH), F32),
                   jax.ShapeDtypeStruct((t // dilation, dilation * LANES), F32)],
        scratch_shapes=[pltpu.VMEM((rows + blk, A_WIDTH), BF16), pltpu.VMEM((rows + blk, A_WIDTH), BF16)],
        compiler_params=_cparams(("arbitrary", "arbitrary", "arbitrary"),
                                 2 * (3 * rows + 2 * blk) * A_WIDTH * 2 + 2 * rows * (A_WIDTH + LANES) * 4
                                 + 2 * (rows + blk) * A_WIDTH * 2 + 16 * blk * 2 * blk * 4),
    )(view, view, view, view, view)
    return o.reshape(t, A_WIDTH), lse.reshape(t, LANES)


def _combine_kernel(o1, o2, o3, l1, l2, l3, out_ref):
    a1, a2, a3 = l1[...], l2[...], l3[...]
    mx = jnp.maximum(jnp.maximum(a1, a2), a3)
    e1, e2, e3 = jnp.exp2(a1 - mx), jnp.exp2(a2 - mx), jnp.exp2(a3 - mx)
    inv = 1.0 / (e1 + e2 + e3)
    w1, w2, w3 = e1 * inv, e2 * inv, e3 * inv
    for hd in range(A_HEADS):
        cs = slice(hd * HEAD_DIM, (hd + 1) * HEAD_DIM)
        acc = (w1[:, hd:hd + 1] * o1[:, cs] + w2[:, hd:hd + 1] * o2[:, cs] + w3[:, hd:hd + 1] * o3[:, cs])
        out_ref[:, cs] = acc.astype(out_ref.dtype)


def _combine(outs, lses):
    t = outs[0].shape[0]
    bm = _tile(t, (512, 256, 128))
    ob = pl.BlockSpec((bm, A_WIDTH), lambda i: (i, 0))
    lb = pl.BlockSpec((bm, LANES), lambda i: (i, 0))
    return pl.pallas_call(
        _combine_kernel,
        name="dilated_combine",
        grid=(t // bm,),
        in_specs=[ob, ob, ob, lb, lb, lb],
        out_specs=ob,
        out_shape=jax.ShapeDtypeStruct((t, A_WIDTH), BF16),
        compiler_params=_cparams(("arbitrary",), 2 * bm * (3 * A_WIDTH * 4 + 3 * LANES * 4 + A_WIDTH * 2) + 8 * bm * LANES * 4),
    )(*outs, *lses)


def _merge_kernel(h_ref, oa_ref, ob_ref, oc_ref, wg0, wg1, wg2, wa, wb, wc, o_ref):
    h = h_ref[...]
    acc = jax.nn.sigmoid(_dot(h, wg0[...])) * _dot(oa_ref[...], wa[...])
    acc += jax.nn.sigmoid(_dot(h, wg1[...])) * _dot(ob_ref[...], wb[...])
    acc += jax.nn.sigmoid(_dot(h, wg2[...])) * _dot(oc_ref[...], wc[...])
    o_ref[...] = acc.astype(o_ref.dtype)


def _merge(h, oa, ob, oc, w_gate, w_a, w_b, w_c):
    m, d = h.shape
    bm = _tile(m, (512, 256, 128))
    bn = _tile(d, (512, 256, 128))
    nj = d // bn

    def rows(width):
        return pl.BlockSpec((bm, width), lambda i, j: (i, 0))

    def gate(g):
        return pl.BlockSpec((d, bn), lambda i, j: (0, g * nj + j))

    def branch(width):
        return pl.BlockSpec((width, bn), lambda i, j: (0, j))

    k_all = 3 * d + A_WIDTH + B_WIDTH + C_WIDTH
    return pl.pallas_call(
        _merge_kernel,
        name="gate_merge",
        grid=(m // bm, nj),
        in_specs=[rows(d), rows(A_WIDTH), rows(B_WIDTH), rows(C_WIDTH), gate(0), gate(1), gate(2),
                  branch(A_WIDTH), branch(B_WIDTH), branch(C_WIDTH)],
        out_specs=pl.BlockSpec((bm, bn), lambda i, j: (i, j)),
        out_shape=jax.ShapeDtypeStruct((m, d), BF16),
        compiler_params=_cparams(("arbitrary", "arbitrary"),
                                 2 * (bm * (d + A_WIDTH + B_WIDTH + C_WIDTH) * 2 + k_all * bn * 2 + bm * bn * 2)
                                 + 8 * bm * bn * 4),
    )(h, oa, ob, oc, w_gate, w_gate, w_gate, w_a, w_b, w_c)


def _ffn_in_kernel(h_ref, wa_ref, wb_ref, o_ref):
    h = h_ref[...]
    a = _dot(h, wa_ref[...])
    b = _dot(h, wb_ref[...])
    o_ref[...] = (a * jax.nn.sigmoid(a) * b).astype(o_ref.dtype)


def _ffn_in(h, w_in, d_ff):
    m, d = h.shape
    bm = _tile(m, (1024, 512, 256, 128))
    bn = _tile(d_ff, (512, 256, 128))
    nj = d_ff // bn
    return pl.pallas_call(
        _ffn_in_kernel,
        name="ffn_in",
        grid=(m // bm, nj),
        in_specs=[pl.BlockSpec((bm, d), lambda i, j: (i, 0)),
                  pl.BlockSpec((d, bn), lambda i, j: (0, j)),
                  pl.BlockSpec((d, bn), lambda i, j: (0, nj + j))],
        out_specs=pl.BlockSpec((bm, bn), lambda i, j: (i, j)),
        out_shape=jax.ShapeDtypeStruct((m, d_ff), BF16),
        compiler_params=_cparams(("arbitrary", "arbitrary"),
                                 2 * (bm * d * 2 + 2 * d * bn * 2 + bm * bn * 2) + 4 * bm * bn * 4),
    )(h, w_in, w_in)


def _layer(x2, h, ada_l, ada_next, batch, seq, tabs_a, tabs_b, w, alpha, last):
    d = x2.shape[1]
    t = batch * seq
    sh1, sc1, gt1, sh2, sc2, gt2 = [a.reshape(batch, 1, d) for a in jnp.split(ada_l, 6, axis=-1)]
    del sh1, sc1

    qkv_a = _proj_ac(h, w["a"], tabs_a)
    qkv_c = _matmul(h, w["c"], BF16, bn_prefs=(768, 512, 256, 128))
    q_b = _mla_q(h, w["cq"], w["g_qn"], w["uq"], tabs_b)
    k_b, v_b, f_logit = _mla_kv(h, w["ckv"], w["g_kvn"], w["ukv"], w["b_f"], tabs_b)

    outs, lses = [], []
    for window, dilation in A_PATTERNS:
        o, l = _dilated_group(qkv_a, qkv_a.shape[1], batch, seq, window, dilation)
        outs.append(o)
        lses.append(l)
    o_a = _combine(outs, lses)

    o_b = _flash(q_b, k_b, v_b, 0, 0, 0, B_QK_PAD, V_DIM, B_HEADS, batch, seq)

    f_t = f_logit[:, :F_ROWS].reshape(batch, seq, F_ROWS).transpose(0, 2, 1)
    c_t = _fox_cumsum(f_t)
    c_row = c_t.reshape(batch * F_ROWS, 1, seq)
    c_rep = jnp.broadcast_to(c_t[:, :C_HEADS, :, None], (batch, C_HEADS, seq, LANES))
    c_rep = c_rep.reshape(batch * C_HEADS, seq, LANES)
    nh = A_WIDTH // HEAD_DIM
    o_c = _flash(qkv_c, qkv_c, qkv_c, 0, nh, 2 * nh, HEAD_DIM, HEAD_DIM, C_HEADS, batch, seq,
                 bias=(c_row, c_rep))

    merged = _merge(h, o_a, o_b, o_c, w["gate"], w["br_a"], w["br_b"], w["br_c"])
    y = _matmul(merged, w["o"], F32)
    x2, h2 = _res_ln(x2, y, gt1, w["ln1_g"], w["ln1_b"], sc2, sh2, batch, alpha, True)

    act = _ffn_in(h2, w["ffn_in"], w["d_ff"])
    nk = 2 if (w["d_ff"] // 2) % LANES == 0 else 1
    y = _matmul_ksplit(act, w["ffn_out"], F32, nk)
    if last:
        x2, hn = _res_ln(x2, y, gt2, w["ln2_g"], w["ln2_b"], sc2, sh2, batch, alpha, False)
    else:
        sh1n, sc1n = [a.reshape(batch, 1, d) for a in jnp.split(ada_next, 6, axis=-1)[:2]]
        x2, hn = _res_ln(x2, y, gt2, w["ln2_g"], w["ln2_b"], sc1n, sh1n, batch, alpha, True)
    return x2, hn


def _prep_weights(l, w_in, b_f, g_qn, w_uq, g_kvn, w_ukv, w_br_a, w_br_b, w_br_c, w_o,
                  ln1_g, ln1_b, w_ffn_in, w_ffn_out, ln2_g, ln2_b):
    d = w_in.shape[1]
    ql = g_qn.shape[1]
    kvl = g_kvn.shape[1]
    wi = w_in[l]
    widths = (A_WIDTH, A_WIDTH, A_WIDTH, ql, kvl, QK_ROPE, C_WIDTH, C_WIDTH, C_WIDTH, C_HEADS, N_BRANCH * d)
    offs = np.concatenate([[0], np.cumsum(widths)])
    qa, ka, va, cq, ckv, kr, qc, kc, vc, fl, gate = [wi[:, offs[i]:offs[i + 1]] for i in range(len(widths))]
    scale_a = HEAD_DIM ** -0.5 * LOG2E
    scale_b = (QK_NOPE + QK_ROPE) ** -0.5 * LOG2E
    w_a = jnp.concatenate([qa * scale_a, ka, va], axis=1).astype(BF16)
    w_c = jnp.concatenate([qc * scale_a, kc, vc], axis=1).astype(BF16)
    zpad = lambda n: jnp.zeros((d, n), F32)
    w_ckv = jnp.concatenate([ckv, kr, zpad(LANES - QK_ROPE), fl, zpad(LANES - C_HEADS)], axis=1).astype(BF16)
    uq = w_uq[l].reshape(ql, B_HEADS, QK_NOPE + QK_ROPE) * scale_b
    uq = jnp.pad(uq, ((0, 0), (0, 0), (0, B_QK_PAD - QK_NOPE - QK_ROPE))).reshape(ql, B_HEADS * B_QK_PAD)
    ukv = w_ukv[l].reshape(kvl, B_HEADS, 2, QK_NOPE).transpose(0, 2, 1, 3).reshape(kvl, 2 * B_HEADS * QK_NOPE)
    b_f_row = jnp.pad(b_f[l], (0, LANES - C_HEADS)).reshape(1, LANES)
    return dict(a=w_a, c=w_c, cq=cq.astype(BF16), g_qn=g_qn[l], uq=uq.astype(BF16), ckv=w_ckv, g_kvn=g_kvn[l],
                ukv=ukv.astype(BF16), b_f=b_f_row, gate=gate.astype(BF16), br_a=w_br_a[l].astype(BF16),
                br_b=w_br_b[l].astype(BF16), br_c=w_br_c[l].astype(BF16), o=w_o[l].astype(BF16),
                ln1_g=ln1_g[l], ln1_b=ln1_b[l], ffn_in=w_ffn_in[l].astype(BF16),
                ffn_out=w_ffn_out[l].astype(BF16), d_ff=w_ffn_out.shape[1], ln2_g=ln2_g[l], ln2_b=ln2_b[l])


def kernel(x, c, positions, w_ada, b_ada, w_in, b_f, g_qn, w_uq, g_kvn, w_ukv, w_br_a, w_br_b, w_br_c, w_o,
           ln1_g, ln1_b, w_ffn_in, w_ffn_out, ln2_g, ln2_b):
    batch, seq, d = x.shape
    depth = w_ada.shape[0]
    alpha = (2.0 * depth) ** 0.25
    t = batch * seq
    x2 = x.reshape(t, d)
    ada = _ada(c, w_ada, b_ada)
    pos_col = positions.astype(F32).reshape(t, 1)
    tabs_a = _rope_tables(pos_col, PARTIAL_ROPE_DIM)
    tabs_b = _rope_tables(pos_col, QK_ROPE)
    sh1, sc1 = [a.reshape(batch, 1, d) for a in jnp.split(ada[0], 6, axis=-1)[:2]]
    h = _modulate(x2, sc1, sh1, batch)
    for l in range(depth):
        w = _prep_weights(l, w_in, b_f, g_qn, w_uq, g_kvn, w_ukv, w_br_a, w_br_b, w_br_c, w_o,
                          ln1_g, ln1_b, w_ffn_in, w_ffn_out, ln2_g, ln2_b)
        last = l == depth - 1
        x2, h = _layer(x2, h, ada[l], None if last else ada[l + 1], batch, seq, tabs_a, tabs_b, w, alpha, last)
    return x2.reshape(batch, seq, d)
```

```python
import functools
import math

import jax
import jax.numpy as jnp
import numpy as np
from jax import lax
from jax.experimental import pallas as pl
from jax.experimental.pallas import tpu as pltpu

HEAD_DIM = 128
ROPE_THETA = 500000.0
PARTIAL_ROPE_DIM = HEAD_DIM // 4
A_HEADS = 12
A_PATTERNS = ((128, 1), (512, 4), (2048, 16))
B_HEADS = 8
QK_NOPE = 128
QK_ROPE = 64
V_DIM = 128
C_HEADS = 12
N_BRANCH = 3
A_WIDTH = A_HEADS * HEAD_DIM
B_WIDTH = B_HEADS * V_DIM
C_WIDTH = C_HEADS * HEAD_DIM
B_QK_PAD = 256
F_ROWS = 16
FLASH_BQ = 1024
FLASH_BK = 1024
NEG = -1e30
LOG2E = math.log2(math.e)
LANES = 128
V7X_VMEM_CAP = 60 * 1024 * 1024

BF16 = jnp.bfloat16
F32 = jnp.float32


def _cparams(semantics, vmem_estimate):
    limit = int(min(max(vmem_estimate * 5 // 4, 32 * 1024 * 1024), V7X_VMEM_CAP))
    return pltpu.CompilerParams(dimension_semantics=semantics, vmem_limit_bytes=limit)


def _tile(n, prefs):
    for p in prefs:
        if n % p == 0:
            return p
    return n


def _resident(shape):
    return pl.BlockSpec(shape, lambda i: (0,) * len(shape), pipeline_mode=pl.Buffered(1))


def _dot(a, b):
    return jnp.dot(a, b, preferred_element_type=F32)


def _dot_nt(a, b):
    return lax.dot_general(a, b, (((1,), (1,)), ((), ())), preferred_element_type=F32)


def _rotate(t, cos, sin_lo, sin_hi, half):
    return t * cos + pltpu.roll(t, half, 1) * sin_hi + pltpu.roll(t, LANES - half, 1) * sin_lo


def _ada_kernel(c_ref, w_ref, b_ref, o_ref, acc_sc):
    kblk = pl.program_id(1)
    nb, kb = c_ref.shape[0], c_ref.shape[1]

    @pl.when(kblk == 0)
    def _():
        acc_sc[...] = jnp.zeros_like(acc_sc)

    cv = c_ref[...]
    s = cv * jax.nn.sigmoid(cv)
    for g in range(w_ref.shape[1] // LANES):
        cols = slice(g * LANES, (g + 1) * LANES)
        wg = w_ref[:, cols]
        for b in range(nb):
            acc_sc[b, :, cols] += jnp.sum((wg * s[b]).reshape(kb // 8, 8, LANES), axis=0)

    @pl.when(kblk == pl.num_programs(1) - 1)
    def _():
        for b in range(nb):
            o_ref[b:b + 1, :] = jnp.sum(acc_sc[b], axis=0, keepdims=True) + b_ref[...]


def _ada(c, w_ada, b_ada):
    depth, d, n = w_ada.shape
    b = c.shape[0]
    kb = _tile(d, (128, 64, 32, 16, 8))
    c_rep = jnp.broadcast_to(c[:, :, None], (b, d, LANES))
    return pl.pallas_call(
        _ada_kernel,
        name="ada",
        grid=(depth, d // kb),
        in_specs=[pl.BlockSpec((b, kb, LANES), lambda l, k: (0, k, 0)),
                  pl.BlockSpec((None, kb, n), lambda l, k: (l, k, 0)),
                  pl.BlockSpec((None, 1, n), lambda l, k: (l, 0, 0))],
        out_specs=pl.BlockSpec((None, b, n), lambda l, k: (l, 0, 0)),
        out_shape=jax.ShapeDtypeStruct((depth, b, n), F32),
        scratch_shapes=[pltpu.VMEM((b, 8, n), F32)],
        compiler_params=_cparams(("arbitrary", "arbitrary"), 2 * kb * n * 4 + 4 * b * 8 * n * 4),
    )(c_rep, w_ada, b_ada.reshape(depth, 1, n))


def _rope_table_kernel(pos_ref, freq_ref, mc_ref, m1_ref, mlo_ref, mhi_ref, cos_ref, lo_ref, hi_ref):
    ang = pos_ref[...] * freq_ref[...]
    cs = jnp.cos(ang)
    sn = jnp.sin(ang)
    cos_ref[...] = cs * mc_ref[...] + m1_ref[...]
    lo_ref[...] = sn * mlo_ref[...]
    hi_ref[...] = sn * mhi_ref[...]


def _rope_tables(pos_col, rot_dim):
    t = pos_col.shape[0]
    half = rot_dim // 2
    inv = np.exp(-math.log(ROPE_THETA) * np.arange(half, dtype=np.float32) * np.float32(2.0 / rot_dim))
    lane = np.arange(LANES)
    freq = np.where(lane < rot_dim, inv[lane % half], 0.0).astype(np.float32)[None]
    m_cos = (lane < rot_dim).astype(np.float32)[None]
    m_one = (lane >= rot_dim).astype(np.float32)[None]
    m_lo = np.where(lane < half, -1.0, 0.0).astype(np.float32)[None]
    m_hi = np.where((lane >= half) & (lane < rot_dim), 1.0, 0.0).astype(np.float32)[None]
    bm = _tile(t, (2048, 1024, 512, 256, 128))
    row = pl.BlockSpec((1, LANES), lambda i: (0, 0))
    tab = pl.BlockSpec((bm, LANES), lambda i: (i, 0))
    shp = jax.ShapeDtypeStruct((t, LANES), F32)
    return pl.pallas_call(
        _rope_table_kernel,
        name="rope_tables",
        grid=(t // bm,),
        in_specs=[pl.BlockSpec((bm, 1), lambda i: (i, 0)), row, row, row, row, row],
        out_specs=[tab, tab, tab],
        out_shape=[shp, shp, shp],
        compiler_params=_cparams(("arbitrary",), 16 * bm * LANES * 4),
    )(pos_col, jnp.asarray(freq), jnp.asarray(m_cos), jnp.asarray(m_one), jnp.asarray(m_lo), jnp.asarray(m_hi))


def _mod_kernel(x_ref, sc_ref, sh_ref, o_ref):
    o_ref[...] = (x_ref[...] * (1.0 + sc_ref[...]) + sh_ref[...]).astype(o_ref.dtype)


def _modulate(x2, sc, sh, batch):
    t, d = x2.shape
    s = t // batch
    bm = _tile(s, (512, 256, 128))
    nb = s // bm
    vec = pl.BlockSpec((None, 1, d), lambda b, i: (b, 0, 0))
    return pl.pallas_call(
        _mod_kernel,
        name="modulate",
        grid=(batch, nb),
        in_specs=[pl.BlockSpec((bm, d), lambda b, i: (b * nb + i, 0)), vec, vec],
        out_specs=pl.BlockSpec((bm, d), lambda b, i: (b * nb + i, 0)),
        out_shape=jax.ShapeDtypeStruct((t, d), BF16),
        compiler_params=_cparams(("arbitrary", "arbitrary"), 2 * bm * d * 6),
    )(x2, sc, sh)


def _res_ln_kernel(x_ref, y_ref, gt_ref, g_ref, b_ref, sc_ref, sh_ref, xo_ref, *ho_ref, alpha):
    z = alpha * x_ref[...] + (1.0 + gt_ref[...]) * y_ref[...].astype(F32)
    mu = jnp.mean(z, axis=-1, keepdims=True)
    zc = z - mu
    var = jnp.mean(zc * zc, axis=-1, keepdims=True)
    xn = zc * lax.rsqrt(var + 1e-5) * g_ref[...] + b_ref[...]
    xo_ref[...] = xn
    if ho_ref:
        ho_ref[0][...] = (xn * (1.0 + sc_ref[...]) + sh_ref[...]).astype(BF16)


def _res_ln(x2, y2, gt, ln_g, ln_b, sc, sh, batch, alpha, with_h):
    t, d = x2.shape
    s = t // batch
    bm = _tile(s, (256, 128))
    nb = s // bm
    vec = pl.BlockSpec((None, 1, d), lambda b, i: (b, 0, 0))
    par = pl.BlockSpec((1, d), lambda b, i: (0, 0))
    blk = pl.BlockSpec((bm, d), lambda b, i: (b * nb + i, 0))
    out_specs = [blk, blk] if with_h else [blk]
    out_shape = [jax.ShapeDtypeStruct((t, d), F32)]
    if with_h:
        out_shape.append(jax.ShapeDtypeStruct((t, d), BF16))
    outs = pl.pallas_call(
        functools.partial(_res_ln_kernel, alpha=alpha),
        name="res_ln",
        grid=(batch, nb),
        in_specs=[blk, blk, vec, par, par, vec, vec],
        out_specs=out_specs,
        out_shape=out_shape,
        compiler_params=_cparams(("arbitrary", "arbitrary"), 2 * bm * d * 14 + 6 * bm * d * 4),
    )(x2, y2, gt, ln_g.reshape(1, d), ln_b.reshape(1, d), sc, sh)
    return (outs[0], outs[1]) if with_h else (outs[0], None)


def _mm_kernel(x_ref, w_ref, o_ref):
    o_ref[...] = _dot(x_ref[...], w_ref[...]).astype(o_ref.dtype)


def _matmul(x, w, out_dtype, bm_prefs=(1024, 512, 256, 128), bn_prefs=(1024, 512, 256, 128)):
    m, k = x.shape
    n = w.shape[1]
    bm = _tile(m, bm_prefs)
    bn = _tile(n, bn_prefs)
    osz = jnp.dtype(out_dtype).itemsize
    return pl.pallas_call(
        _mm_kernel,
        name="matmul",
        grid=(m // bm, n // bn),
        in_specs=[pl.BlockSpec((bm, k), lambda i, j: (i, 0)),
                  pl.BlockSpec((k, bn), lambda i, j: (0, j))],
        out_specs=pl.BlockSpec((bm, bn), lambda i, j: (i, j)),
        out_shape=jax.ShapeDtypeStruct((m, n), out_dtype),
        compiler_params=_cparams(("arbitrary", "arbitrary"),
                                 2 * (bm * k * 2 + k * bn * 2 + bm * bn * osz) + bm * bn * 4),
    )(x, w)


def _mm_acc_kernel(x_ref, w_ref, o_ref, acc_ref):
    kk = pl.program_id(2)

    @pl.when(kk == 0)
    def _():
        acc_ref[...] = jnp.zeros_like(acc_ref)

    acc_ref[...] += _dot(x_ref[...], w_ref[...])

    @pl.when(kk == pl.num_programs(2) - 1)
    def _():
        o_ref[...] = acc_ref[...].astype(o_ref.dtype)


def _matmul_ksplit(x, w, out_dtype, nk):
    m, k = x.shape
    n = w.shape[1]
    bm = _tile(m, (1024, 512, 256, 128))
    bn = _tile(n, (512, 256, 128))
    bk = k // nk
    osz = jnp.dtype(out_dtype).itemsize
    return pl.pallas_call(
        _mm_acc_kernel,
        name="matmul_ksplit",
        grid=(m // bm, n // bn, nk),
        in_specs=[pl.BlockSpec((bm, bk), lambda i, j, q: (i, q)),
                  pl.BlockSpec((bk, bn), lambda i, j, q: (q, j))],
        out_specs=pl.BlockSpec((bm, bn), lambda i, j, q: (i, j)),
        out_shape=jax.ShapeDtypeStruct((m, n), out_dtype),
        scratch_shapes=[pltpu.VMEM((bm, bn), F32)],
        compiler_params=_cparams(("arbitrary", "arbitrary", "arbitrary"),
                                 2 * (bm * bk * 2 + bk * bn * 2 + bm * bn * osz) + 2 * bm * bn * 4),
    )(x, w)


def _proj_ac_kernel(x_ref, w_ref, cos_ref, lo_ref, hi_ref, o_ref, *, n_rot_tiles, half):
    acc = _dot(x_ref[...], w_ref[...])
    j = pl.program_id(1)

    @pl.when(j < n_rot_tiles)
    def _():
        cs, lo, hi = cos_ref[...], lo_ref[...], hi_ref[...]
        for g in range(acc.shape[1] // LANES):
            sl = slice(g * LANES, (g + 1) * LANES)
            o_ref[:, sl] = _rotate(acc[:, sl], cs, lo, hi, half).astype(o_ref.dtype)

    @pl.when(j >= n_rot_tiles)
    def _():
        o_ref[...] = acc.astype(o_ref.dtype)


def _proj_ac(h, w_ac, tabs):
    m, k = h.shape
    n = w_ac.shape[1]
    bm = _tile(m, (1024, 512, 256, 128))
    bn = _tile(2 * A_WIDTH, (768, 512, 256, 128))
    tab = pl.BlockSpec((bm, LANES), lambda i, j: (i, 0))
    return pl.pallas_call(
        functools.partial(_proj_ac_kernel, n_rot_tiles=2 * A_WIDTH // bn, half=PARTIAL_ROPE_DIM // 2),
        name="proj_ac",
        grid=(m // bm, n // bn),
        in_specs=[pl.BlockSpec((bm, k), lambda i, j: (i, 0)),
                  pl.BlockSpec((k, bn), lambda i, j: (0, j)), tab, tab, tab],
        out_specs=pl.BlockSpec((bm, bn), lambda i, j: (i, j)),
        out_shape=jax.ShapeDtypeStruct((m, n), BF16),
        compiler_params=_cparams(("arbitrary", "arbitrary"),
                                 2 * (bm * k * 2 + k * bn * 2 + bm * bn * 2 + 3 * bm * LANES * 4) + 2 * bm * bn * 4),
    )(h, w_ac, *tabs)


def _rms(x, g, eps=1e-6):
    return x * lax.rsqrt(jnp.mean(x * x, axis=-1, keepdims=True) + eps) * g


def _mla_q_kernel(h_ref, wcq_ref, g_ref, wuq_ref, cos_ref, lo_ref, hi_ref, o_ref):
    cq = _dot(h_ref[...], wcq_ref[...])
    q = _dot(_rms(cq, g_ref[...]).astype(BF16), wuq_ref[...])
    cs, lo, hi = cos_ref[...], lo_ref[...], hi_ref[...]
    for hd in range(B_HEADS):
        base = hd * B_QK_PAD
        o_ref[:, base:base + QK_NOPE] = q[:, base:base + QK_NOPE].astype(BF16)
        rope = _rotate(q[:, base + QK_NOPE:base + B_QK_PAD], cs, lo, hi, QK_ROPE // 2)
        o_ref[:, base + QK_NOPE:base + B_QK_PAD] = rope.astype(BF16)


def _mla_q(h, w_cq, g_qn, w_uq, tabs):
    m, k = h.shape
    ql = w_cq.shape[1]
    n = w_uq.shape[1]
    bm = _tile(m, (512, 256, 128))
    tab = pl.BlockSpec((bm, LANES), lambda i: (i, 0))
    return pl.pallas_call(
        _mla_q_kernel,
        name="mla_q",
        grid=(m // bm,),
        in_specs=[pl.BlockSpec((bm, k), lambda i: (i, 0)),
                  _resident((k, ql)), _resident((1, ql)), _resident((ql, n)), tab, tab, tab],
        out_specs=pl.BlockSpec((bm, n), lambda i: (i, 0)),
        out_shape=jax.ShapeDtypeStruct((m, n), BF16),
        compiler_params=_cparams(("arbitrary",),
                                 2 * (bm * k * 2 + bm * n * 2) + k * ql * 2 + ql * n * 2 + bm * (ql + n) * 8),
    )(h, w_cq, g_qn.reshape(1, ql), w_uq, *tabs)


def _mla_kv_kernel(h_ref, wc_ref, g_ref, wukv_ref, bf_ref, cos_ref, lo_ref, hi_ref, k_ref, v_ref, f_ref, *, kvl):
    ck = _dot(h_ref[...], wc_ref[...])
    kv = _dot(_rms(ck[:, :kvl], g_ref[...]).astype(BF16), wukv_ref[...])
    kr = _rotate(ck[:, kvl:kvl + LANES], cos_ref[...], lo_ref[...], hi_ref[...], QK_ROPE // 2).astype(BF16)
    for hd in range(B_HEADS):
        base = hd * B_QK_PAD
        k_ref[:, base:base + QK_NOPE] = kv[:, hd * QK_NOPE:(hd + 1) * QK_NOPE].astype(BF16)
        k_ref[:, base + QK_NOPE:base + B_QK_PAD] = kr
    v_ref[...] = kv[:, B_HEADS * QK_NOPE:].astype(BF16)
    f_ref[...] = ck[:, kvl + LANES:] + bf_ref[...]


def _mla_kv(h, w_c, g_kvn, w_ukv, b_f_row, tabs):
    m, k = h.shape
    kvl = g_kvn.shape[0]
    nc = w_c.shape[1]
    bm = _tile(m, (512, 256, 128))
    tab = pl.BlockSpec((bm, LANES), lambda i: (i, 0))
    nk = B_HEADS * B_QK_PAD
    return pl.pallas_call(
        functools.partial(_mla_kv_kernel, kvl=kvl),
        name="mla_kv",
        grid=(m // bm,),
        in_specs=[pl.BlockSpec((bm, k), lambda i: (i, 0)),
                  _resident((k, nc)), _resident((1, kvl)), _resident(w_ukv.shape), _resident((1, LANES)),
                  tab, tab, tab],
        out_specs=[pl.BlockSpec((bm, nk), lambda i: (i, 0)),
                   pl.BlockSpec((bm, B_WIDTH), lambda i: (i, 0)),
                   pl.BlockSpec((bm, LANES), lambda i: (i, 0))],
        out_shape=[jax.ShapeDtypeStruct((m, nk), BF16),
                   jax.ShapeDtypeStruct((m, B_WIDTH), BF16),
                   jax.ShapeDtypeStruct((m, LANES), F32)],
        compiler_params=_cparams(("arbitrary",),
                                 2 * (bm * k * 2 + k * nc * 2 + w_ukv.size * 2 + bm * (nk + B_WIDTH) * 2)
                                 + bm * (nc + nk + B_WIDTH) * 8),
    )(h, w_c, g_kvn.reshape(1, kvl), w_ukv, b_f_row, *tabs)


def _fox_cumsum_kernel(f_ref, o_ref):
    x = f_ref[...]
    y = (jnp.minimum(x, 0.0) - jnp.log(1.0 + jnp.exp(-jnp.abs(x)))) * LOG2E
    s = y.shape[1]
    lane = lax.broadcasted_iota(jnp.int32, y.shape, 1)
    shift = 1
    while shift < s:
        y = y + jnp.where(lane >= shift, pltpu.roll(y, shift, 1), 0.0)
        shift *= 2
    o_ref[...] = y


def _fox_cumsum(f_t):
    b, r, s = f_t.shape
    return pl.pallas_call(
        _fox_cumsum_kernel,
        name="fox_cumsum",
        grid=(b,),
        in_specs=[pl.BlockSpec((None, r, s), lambda i: (i, 0, 0))],
        out_specs=pl.BlockSpec((None, r, s), lambda i: (i, 0, 0)),
        out_shape=jax.ShapeDtypeStruct((b, r, s), F32),
        compiler_params=_cparams(("arbitrary",), 8 * r * s * 4),
    )(f_t)


def _flash_kernel(*refs, bq, bk, has_bias):
    if has_bias:
        q_ref, k_ref, v_ref, cq_ref, ck_ref, o_ref, acc_sc, s_sc, p_sc = refs
    else:
        q_ref, k_ref, v_ref, o_ref, acc_sc, s_sc, p_sc = refs
    n_diag = bq // bk
    n_full = pl.program_id(2) * n_diag
    fused_max = not has_bias

    def causal(s, diag):
        key = lax.broadcasted_iota(jnp.int32, (bk, bq), 0) + diag * bk
        qry = lax.broadcasted_iota(jnp.int32, (bk, bq), 1)
        return jnp.where(key <= qry, s, NEG)

    def scores(j, slot, diag=None):
        start = pl.multiple_of(j * bk, bk)
        s = _dot_nt(k_ref[pl.ds(start, bk), :], q_ref[...])
        if has_bias:
            ck = ck_ref[pl.ds(start, bk), :]
            s = s + cq_ref[...] - jnp.concatenate([ck] * (bq // LANES), axis=1)
        if diag is not None:
            s = causal(s, diag)
        s_sc[slot] = s
        return jnp.max(s, axis=0, keepdims=True) if fused_max else jnp.zeros((1, bq), F32)

    def accumulate(j, alpha, slot):
        start = pl.multiple_of(j * bk, bk)
        pv = lax.dot_general(v_ref[pl.ds(start, bk), :], p_sc[slot], (((0,), (0,)), ((), ())),
                             preferred_element_type=F32)
        acc_sc[...] = alpha * acc_sc[...] + pv

    def softmax_update(m_prev, l_prev, block_max, slot):
        if not fused_max:
            block_max = jnp.max(s_sc[slot], axis=0, keepdims=True)
        m_new = jnp.maximum(m_prev, block_max)
        alpha = jnp.exp2(m_prev - m_new)
        p = jnp.exp2(s_sc[slot] - m_new)
        l_new = alpha * l_prev + jnp.sum(p, axis=0, keepdims=True)
        p_sc[slot] = p.astype(BF16)
        return m_new, l_new, alpha

    acc_sc[...] = jnp.zeros_like(acc_sc)
    p_sc[1] = jnp.zeros((bk, bq), BF16)
    bmax = scores(0, 0)

    def body(j, carry):
        m, l, alpha_prev, bmax = carry
        cur = j % 2
        accumulate(jnp.maximum(j - 1, 0), alpha_prev, 1 - cur)
        m, l, alpha = softmax_update(m, l, bmax, cur)
        return m, l, alpha, scores(j + 1, 1 - cur)

    init = (jnp.full((1, bq), NEG, F32), jnp.zeros((1, bq), F32), jnp.ones((1, bq), F32), bmax)
    m, l, alpha, bmax = lax.fori_loop(0, n_full, body, init)
    cur = n_full % 2
    s = causal(s_sc[cur], 0)
    s_sc[cur] = s
    bmax = jnp.max(s, axis=0, keepdims=True)
    for t in range(n_diag):
        j = n_full + t
        cur = j % 2
        accumulate(jnp.maximum(j - 1, 0), alpha, 1 - cur)
        m, l, alpha = softmax_update(m, l, bmax, cur)
        if t + 1 < n_diag:
            bmax = scores(j + 1, 1 - cur, t + 1)
    accumulate(n_full + n_diag - 1, alpha, cur)
    o_ref[...] = jnp.transpose(acc_sc[...] / l).astype(o_ref.dtype)


def _flash(q_arr, k_arr, v_arr, q_col0, k_col0, v_col0, dq, dv, heads, batch, seq, bias=None):
    bq = _tile(seq, (FLASH_BQ, 512, 256, 128))
    bk = _tile(bq, (FLASH_BK, 256, 128))
    nq = seq // bq
    t = batch * seq
    in_specs = [pl.BlockSpec((bq, dq), lambda b, h, i: (b * nq + i, q_col0 + h)),
                pl.BlockSpec((seq, dq), lambda b, h, i: (b, k_col0 + h)),
                pl.BlockSpec((seq, dv), lambda b, h, i: (b, v_col0 + h))]
    args = [q_arr, k_arr, v_arr]
    if bias is not None:
        c_row, c_rep = bias
        in_specs += [pl.BlockSpec((None, 1, bq), lambda b, h, i: (b * F_ROWS + h, 0, i)),
                     pl.BlockSpec((None, seq, LANES), lambda b, h, i: (b * heads + h, 0, 0))]
        args += [c_row, c_rep]
    return pl.pallas_call(
        functools.partial(_flash_kernel, bq=bq, bk=bk, has_bias=bias is not None),
        name="flash_fox" if bias is not None else "flash_mla",
        grid=(batch, heads, nq),
        in_specs=in_specs,
        out_specs=pl.BlockSpec((bq, dv), lambda b, h, i: (b * nq + i, h)),
        out_shape=jax.ShapeDtypeStruct((t, heads * dv), BF16),
        scratch_shapes=[pltpu.VMEM((dv, bq), F32), pltpu.VMEM((2, bk, bq), F32), pltpu.VMEM((2, bk, bq), BF16)],
        compiler_params=_cparams(("arbitrary", "arbitrary", "arbitrary"),
                                 2 * (seq * (dq + dv) * 2 + bq * (dq + dv) * 2 + seq * LANES * 4)
                                 + 8 * bq * bk * 4),
    )(*args)


def _dilated_kernel(q_ref, kp_ref, kc_ref, vp_ref, vc_ref, o_ref, lse_ref, kband, vband, *, rows, blk):
    n = pl.program_id(2)
    kband[0:blk, :] = kp_ref[...]
    kband[blk:, :] = kc_ref[...]
    vband[0:blk, :] = vp_ref[...]
    vband[blk:, :] = vc_ref[...]
    qi = lax.broadcasted_iota(jnp.int32, (blk, 2 * blk), 0)
    ki = lax.broadcasted_iota(jnp.int32, (blk, 2 * blk), 1)
    window = (ki >= qi) & (ki <= qi + blk)
    lane = lax.broadcasted_iota(jnp.int32, (blk, LANES), 1)

    def sub_block(a, carry):
        ro = pl.multiple_of(a * blk, blk)
        first_key = jnp.where(n * rows + ro > 0, 0, blk)
        mask = window & (ki >= first_key)
        lse_tile = jnp.zeros((blk, LANES), F32)
        for hd in range(A_HEADS):
            cs = slice(hd * HEAD_DIM, (hd + 1) * HEAD_DIM)
            s = _dot_nt(q_ref[pl.ds(ro, blk), cs], kband[pl.ds(ro, 2 * blk), cs])
            s = jnp.where(mask, s, NEG)
            m = jnp.max(s, axis=-1, keepdims=True)
            p = jnp.exp2(s - m)
            l = jnp.sum(p, axis=-1, keepdims=True)
            o = _dot(p.astype(BF16), vband[pl.ds(ro, 2 * blk), cs]) / l
            o_ref[pl.ds(ro, blk), cs] = o.astype(o_ref.dtype)
            lse_tile = jnp.where(lane == hd, m + jnp.log2(l), lse_tile)
        lse_ref[pl.ds(ro, blk), :] = lse_tile
        return carry

    lax.fori_loop(0, rows // blk, sub_block, 0)


def _dilated_group(qkv, row_width, batch, seq, window, dilation):
    blk = window // dilation
    sub = seq // dilation
    assert sub % blk == 0 and row_width % A_WIDTH == 0
    rows = _tile(sub, (4 * blk, 2 * blk, blk))
    nb = sub // rows
    rpb = rows // blk
    cpr = row_width // A_WIDTH
    t = batch * seq
    view = qkv.reshape(t // dilation, dilation * row_width)

    def cur(c):
        return pl.BlockSpec((rows, A_WIDTH), lambda b, r, n: (b * nb + n, cpr * r + c))

    def prev(c):
        return pl.BlockSpec((blk, A_WIDTH),
                            lambda b, r, n: (b * (sub // blk) + jnp.maximum(n * rpb - 1, 0), cpr * r + c))

    o, lse = pl.pallas_call(
        functools.partial(_dilated_kernel, rows=rows, blk=blk),
        name=f"dilated_d{dilation}",
        grid=(batch, dilation, nb),
        in_specs=[cur(0), prev(1), cur(1), prev(2), cur(2)],
        out_specs=[pl.BlockSpec((rows, A_WIDTH), lambda b, r, n: (b * nb + n, r)),
                   pl.BlockSpec((rows, LANES), lambda b, r, n: (b * nb + n, r))],
        out_shape=[jax.ShapeDtypeStruct((t // dilation, dilation * A_WIDTH), F32),
                   jax.ShapeDtypeStruct((t // dilation, dilation * LANES), F32)],
        scratch_shapes=[pltpu.VMEM((rows + blk, A_WIDTH), BF16), pltpu.VMEM((rows + blk, A_WIDTH), BF16)],
        compiler_params=_cparams(("arbitrary", "arbitrary", "arbitrary"),
                                 2 * (3 * rows + 2 * blk) * A_WIDTH * 2 + 2 * rows * (A_WIDTH + LANES) * 4
                                 + 2 * (rows + blk) * A_WIDTH * 2 + 16 * blk * 2 * blk * 4),
    )(view, view, view, view, view)
    return o.reshape(t, A_WIDTH), lse.reshape(t, LANES)


DEINTERLEAVE_ROWS = 16 * max(d for _, d in A_PATTERNS)


def _deinterleave_kernel(x_ref, *refs, dilations):
    out_refs, scr = refs[:-1], refs[-1]
    rows = scr.shape[1]
    for g in range(scr.shape[0]):
        cols = slice(g * LANES, (g + 1) * LANES)
        scr[g] = x_ref[:, cols].astype(F32)
        for o_ref, d in zip(out_refs, dilations):
            for r in range(d):
                o_ref[r, :, cols] = scr[g, pl.ds(r, rows // d, stride=d), :].astype(o_ref.dtype)


def _deinterleave(x, batch, seq, dilations):
    t, w = x.shape
    rows = DEINTERLEAVE_ROWS
    nb = seq // rows
    outs = pl.pallas_call(
        functools.partial(_deinterleave_kernel, dilations=dilations),
        name="dilated_deinterleave",
        grid=(batch, nb),
        in_specs=[pl.BlockSpec((rows, w), lambda b, n: (b * nb + n, 0))],
        out_specs=[pl.BlockSpec((None, d, rows // d, w), lambda b, n: (b, 0, n, 0)) for d in dilations],
        out_shape=[jax.ShapeDtypeStruct((batch, d, seq // d, w), x.dtype) for d in dilations],
        scratch_shapes=[pltpu.VMEM((w // LANES, rows, LANES), F32)],
        compiler_params=_cparams(("arbitrary", "arbitrary"), rows * w * (4 + 4 + 4 * len(dilations) + 8)),
    )(x)
    return [o.reshape(t, w) for o in outs]


def _combine_kernel(*refs, dilations):
    n = len(dilations)
    o_refs, l_refs, out_ref, scratch = refs[:n], refs[n:2 * n], refs[2 * n], refs[2 * n + 1:]
    heads, lses = [], []
    for o_ref, l_ref, d in zip(o_refs, l_refs, dilations):
        if d == 1:
            heads.append(lambda hd, o_ref=o_ref: o_ref[:, hd * HEAD_DIM:(hd + 1) * HEAD_DIM])
            lses.append(l_ref[...])
            continue
        o_sc, l_sc = scratch[:2]
        scratch = scratch[2:]
        rows = o_sc.shape[1]
        for r in range(d):
            dst = pl.ds(r, rows // d, stride=d)
            l_sc[0, dst, :] = l_ref[r]
            for hd in range(A_HEADS):
                o_sc[hd, dst, :] = o_ref[r, :, hd * HEAD_DIM:(hd + 1) * HEAD_DIM]
        heads.append(lambda hd, o_sc=o_sc: o_sc[hd])
        lses.append(l_sc[0])
    mx = functools.reduce(jnp.maximum, lses)
    es = [jnp.exp2(a - mx) for a in lses]
    inv = 1.0 / functools.reduce(jnp.add, es)
    ws = [e * inv for e in es]
    for hd in range(A_HEADS):
        acc = functools.reduce(jnp.add, [w[:, hd:hd + 1] * head(hd) for w, head in zip(ws, heads)])
        out_ref[:, hd * HEAD_DIM:(hd + 1) * HEAD_DIM] = acc.astype(out_ref.dtype)


def _combine(outs, lses, batch, seq, dilations):
    t = batch * seq
    rows = DEINTERLEAVE_ROWS
    nb = seq // rows
    in_specs, args, scratch = [], [], []
    for width, arrs in ((A_WIDTH, outs), (LANES, lses)):
        for a, d in zip(arrs, dilations):
            if d == 1:
                in_specs.append(pl.BlockSpec((rows, width), lambda b, n: (b * nb + n, 0)))
                args.append(a)
            else:
                in_specs.append(pl.BlockSpec((None, d, rows // d, width), lambda b, n: (b, 0, n, 0)))
                args.append(a.reshape(batch, d, seq // d, width))
    for d in dilations:
        if d != 1:
            scratch += [pltpu.VMEM((A_HEADS, rows, LANES), F32), pltpu.VMEM((1, rows, LANES), F32)]
    return pl.pallas_call(
        functools.partial(_combine_kernel, dilations=dilations),
        name="dilated_combine",
        grid=(batch, nb),
        in_specs=in_specs,
        out_specs=pl.BlockSpec((rows, A_WIDTH), lambda b, n: (b * nb + n, 0)),
        out_shape=jax.ShapeDtypeStruct((t, A_WIDTH), BF16),
        scratch_shapes=scratch,
        compiler_params=_cparams(("arbitrary", "arbitrary"),
                                 rows * (A_WIDTH + LANES) * 4 * (3 * len(dilations) + 4)),
    )(*args)


def _merge_kernel(h_ref, oa_ref, ob_ref, oc_ref, wg0, wg1, wg2, wa, wb, wc, o_ref):
    h = h_ref[...]
    acc = jax.nn.sigmoid(_dot(h, wg0[...])) * _dot(oa_ref[...], wa[...])
    acc += jax.nn.sigmoid(_dot(h, wg1[...])) * _dot(ob_ref[...], wb[...])
    acc += jax.nn.sigmoid(_dot(h, wg2[...])) * _dot(oc_ref[...], wc[...])
    o_ref[...] = acc.astype(o_ref.dtype)


def _merge(h, oa, ob, oc, w_gate, w_a, w_b, w_c):
    m, d = h.shape
    bm = _tile(m, (512, 256, 128))
    bn = _tile(d, (512, 256, 128))
    nj = d // bn

    def rows(width):
        return pl.BlockSpec((bm, width), lambda i, j: (i, 0))

    def gate(g):
        return pl.BlockSpec((d, bn), lambda i, j: (0, g * nj + j))

    def branch(width):
        return pl.BlockSpec((width, bn), lambda i, j: (0, j))

    k_all = 3 * d + A_WIDTH + B_WIDTH + C_WIDTH
    return pl.pallas_call(
        _merge_kernel,
        name="gate_merge",
        grid=(m // bm, nj),
        in_specs=[rows(d), rows(A_WIDTH), rows(B_WIDTH), rows(C_WIDTH), gate(0), gate(1), gate(2),
                  branch(A_WIDTH), branch(B_WIDTH), branch(C_WIDTH)],
        out_specs=pl.BlockSpec((bm, bn), lambda i, j: (i, j)),
        out_shape=jax.ShapeDtypeStruct((m, d), BF16),
        compiler_params=_cparams(("arbitrary", "arbitrary"),
                                 2 * (bm * (d + A_WIDTH + B_WIDTH + C_WIDTH) * 2 + k_all * bn * 2 + bm * bn * 2)
                                 + 8 * bm * bn * 4),
    )(h, oa, ob, oc, w_gate, w_gate, w_gate, w_a, w_b, w_c)


def _ffn_in_kernel(h_ref, wa_ref, wb_ref, o_ref):
    h = h_ref[...]
    a = _dot(h, wa_ref[...])
    b = _dot(h, wb_ref[...])
    o_ref[...] = (a * jax.nn.sigmoid(a) * b).astype(o_ref.dtype)


def _ffn_in(h, w_in, d_ff):
    m, d = h.shape
    bm = _tile(m, (1024, 512, 256, 128))
    bn = _tile(d_ff, (512, 256, 128))
    nj = d_ff // bn
    return pl.pallas_call(
        _ffn_in_kernel,
        name="ffn_in",
        grid=(m // bm, nj),
        in_specs=[pl.BlockSpec((bm, d), lambda i, j: (i, 0)),
                  pl.BlockSpec((d, bn), lambda i, j: (0, j)),
                  pl.BlockSpec((d, bn), lambda i, j: (0, nj + j))],
        out_specs=pl.BlockSpec((bm, bn), lambda i, j: (i, j)),
        out_shape=jax.ShapeDtypeStruct((m, d_ff), BF16),
        compiler_params=_cparams(("arbitrary", "arbitrary"),
                                 2 * (bm * d * 2 + 2 * d * bn * 2 + bm * bn * 2) + 4 * bm * bn * 4),
    )(h, w_in, w_in)


def _layer(x2, h, ada_l, ada_next, batch, seq, tabs_a, tabs_b, w, alpha, last):
    d = x2.shape[1]
    t = batch * seq
    sh1, sc1, gt1, sh2, sc2, gt2 = [a.reshape(batch, 1, d) for a in jnp.split(ada_l, 6, axis=-1)]
    del sh1, sc1

    qkv_a = _proj_ac(h, w["a"], tabs_a)
    qkv_c = _matmul(h, w["c"], BF16, bn_prefs=(768, 512, 256, 128))
    q_b = _mla_q(h, w["cq"], w["g_qn"], w["uq"], tabs_b)
    k_b, v_b, f_logit = _mla_kv(h, w["ckv"], w["g_kvn"], w["ukv"], w["b_f"], tabs_b)

    dilations = tuple(dil for _, dil in A_PATTERNS)
    strided = [dil for dil in dilations if dil != 1]
    copies = dict(zip(strided, _deinterleave(qkv_a, batch, seq, strided)))
    outs, lses = [], []
    for window, dil in A_PATTERNS:
        src = qkv_a if dil == 1 else copies[dil]
        o, l = _dilated_group(src, src.shape[1], batch * dil, seq // dil, window // dil, 1)
        outs.append(o)
        lses.append(l)
    o_a = _combine(outs, lses, batch, seq, dilations)

    o_b = _flash(q_b, k_b, v_b, 0, 0, 0, B_QK_PAD, V_DIM, B_HEADS, batch, seq)

    f_t = f_logit[:, :F_ROWS].reshape(batch, seq, F_ROWS).transpose(0, 2, 1)
    c_t = _fox_cumsum(f_t)
    c_row = c_t.reshape(batch * F_ROWS, 1, seq)
    c_rep = jnp.broadcast_to(c_t[:, :C_HEADS, :, None], (batch, C_HEADS, seq, LANES))
    c_rep = c_rep.reshape(batch * C_HEADS, seq, LANES)
    nh = A_WIDTH // HEAD_DIM
    o_c = _flash(qkv_c, qkv_c, qkv_c, 0, nh, 2 * nh, HEAD_DIM, HEAD_DIM, C_HEADS, batch, seq,
                 bias=(c_row, c_rep))

    merged = _merge(h, o_a, o_b, o_c, w["gate"], w["br_a"], w["br_b"], w["br_c"])
    y = _matmul(merged, w["o"], F32)
    x2, h2 = _res_ln(x2, y, gt1, w["ln1_g"], w["ln1_b"], sc2, sh2, batch, alpha, True)

    act = _ffn_in(h2, w["ffn_in"], w["d_ff"])
    nk = 2 if (w["d_ff"] // 2) % LANES == 0 else 1
    y = _matmul_ksplit(act, w["ffn_out"], F32, nk)
    if last:
        x2, hn = _res_ln(x2, y, gt2, w["ln2_g"], w["ln2_b"], sc2, sh2, batch, alpha, False)
    else:
        sh1n, sc1n = [a.reshape(batch, 1, d) for a in jnp.split(ada_next, 6, axis=-1)[:2]]
        x2, hn = _res_ln(x2, y, gt2, w["ln2_g"], w["ln2_b"], sc1n, sh1n, batch, alpha, True)
    return x2, hn


def _prep_weights(l, w_in, b_f, g_qn, w_uq, g_kvn, w_ukv, w_br_a, w_br_b, w_br_c, w_o,
                  ln1_g, ln1_b, w_ffn_in, w_ffn_out, ln2_g, ln2_b):
    d = w_in.shape[1]
    ql = g_qn.shape[1]
    kvl = g_kvn.shape[1]
    wi = w_in[l]
    widths = (A_WIDTH, A_WIDTH, A_WIDTH, ql, kvl, QK_ROPE, C_WIDTH, C_WIDTH, C_WIDTH, C_HEADS, N_BRANCH * d)
    offs = np.concatenate([[0], np.cumsum(widths)])
    qa, ka, va, cq, ckv, kr, qc, kc, vc, fl, gate = [wi[:, offs[i]:offs[i + 1]] for i in range(len(widths))]
    scale_a = HEAD_DIM ** -0.5 * LOG2E
    scale_b = (QK_NOPE + QK_ROPE) ** -0.5 * LOG2E
    w_a = jnp.concatenate([qa * scale_a, ka, va], axis=1).astype(BF16)
    w_c = jnp.concatenate([qc * scale_a, kc, vc], axis=1).astype(BF16)
    zpad = lambda n: jnp.zeros((d, n), F32)
    w_ckv = jnp.concatenate([ckv, kr, zpad(LANES - QK_ROPE), fl, zpad(LANES - C_HEADS)], axis=1).astype(BF16)
    uq = w_uq[l].reshape(ql, B_HEADS, QK_NOPE + QK_ROPE) * scale_b
    uq = jnp.pad(uq, ((0, 0), (0, 0), (0, B_QK_PAD - QK_NOPE - QK_ROPE))).reshape(ql, B_HEADS * B_QK_PAD)
    ukv = w_ukv[l].reshape(kvl, B_HEADS, 2, QK_NOPE).transpose(0, 2, 1, 3).reshape(kvl, 2 * B_HEADS * QK_NOPE)
    b_f_row = jnp.pad(b_f[l], (0, LANES - C_HEADS)).reshape(1, LANES)
    return dict(a=w_a, c=w_c, cq=cq.astype(BF16), g_qn=g_qn[l], uq=uq.astype(BF16), ckv=w_ckv, g_kvn=g_kvn[l],
                ukv=ukv.astype(BF16), b_f=b_f_row, gate=gate.astype(BF16), br_a=w_br_a[l].astype(BF16),
                br_b=w_br_b[l].astype(BF16), br_c=w_br_c[l].astype(BF16), o=w_o[l].astype(BF16),
                ln1_g=ln1_g[l], ln1_b=ln1_b[l], ffn_in=w_ffn_in[l].astype(BF16),
                ffn_out=w_ffn_out[l].astype(BF16), d_ff=w_ffn_out.shape[1], ln2_g=ln2_g[l], ln2_b=ln2_b[l])


def kernel(x, c, positions, w_ada, b_ada, w_in, b_f, g_qn, w_uq, g_kvn, w_ukv, w_br_a, w_br_b, w_br_c, w_o,
           ln1_g, ln1_b, w_ffn_in, w_ffn_out, ln2_g, ln2_b):
    batch, seq, d = x.shape
    depth = w_ada.shape[0]
    alpha = (2.0 * depth) ** 0.25
    t = batch * seq
    x2 = x.reshape(t, d)
    ada = _ada(c, w_ada, b_ada)
    pos_col = positions.astype(F32).reshape(t, 1)
    tabs_a = _rope_tables(pos_col, PARTIAL_ROPE_DIM)
    tabs_b = _rope_tables(pos_col, QK_ROPE)
    sh1, sc1 = [a.reshape(batch, 1, d) for a in jnp.split(ada[0], 6, axis=-1)[:2]]
    h = _modulate(x2, sc1, sh1, batch)
    for l in range(depth):
        w = _prep_weights(l, w_in, b_f, g_qn, w_uq, g_kvn, w_ukv, w_br_a, w_br_b, w_br_c, w_o,
                          ln1_g, ln1_b, w_ffn_in, w_ffn_out, ln2_g, ln2_b)
        last = l == depth - 1
        x2, h = _layer(x2, h, ada[l], None if last else ada[l + 1], batch, seq, tabs_a, tabs_b, w, alpha, last)
    return x2.reshape(batch, seq, d)
```

```python
import functools
import math

import jax
import jax.numpy as jnp
import numpy as np
from jax import lax
from jax.experimental import pallas as pl
from jax.experimental.pallas import tpu as pltpu

HEAD_DIM = 128
ROPE_THETA = 500000.0
PARTIAL_ROPE_DIM = HEAD_DIM // 4
A_HEADS = 12
A_PATTERNS = ((128, 1), (512, 4), (2048, 16))
B_HEADS = 8
QK_NOPE = 128
QK_ROPE = 64
V_DIM = 128
C_HEADS = 12
N_BRANCH = 3
A_WIDTH = A_HEADS * HEAD_DIM
B_WIDTH = B_HEADS * V_DIM
C_WIDTH = C_HEADS * HEAD_DIM
B_QK_PAD = 256
F_ROWS = 16
FLASH_BQ = 1024
FLASH_BK = 1024
NEG = -1e30
LOG2E = math.log2(math.e)
LANES = 128
V7X_VMEM_CAP = 60 * 1024 * 1024

BF16 = jnp.bfloat16
F32 = jnp.float32


def _cparams(semantics, vmem_estimate):
    limit = int(min(max(vmem_estimate * 5 // 4, 32 * 1024 * 1024), V7X_VMEM_CAP))
    return pltpu.CompilerParams(dimension_semantics=semantics, vmem_limit_bytes=limit)


def _tile(n, prefs):
    for p in prefs:
        if n % p == 0:
            return p
    return n


def _resident(shape):
    return pl.BlockSpec(shape, lambda i: (0,) * len(shape), pipeline_mode=pl.Buffered(1))


def _dot(a, b):
    return jnp.dot(a, b, preferred_element_type=F32)


def _dot_nt(a, b):
    return lax.dot_general(a, b, (((1,), (1,)), ((), ())), preferred_element_type=F32)


def _rotate(t, cos, sin_lo, sin_hi, half):
    return t * cos + pltpu.roll(t, half, 1) * sin_hi + pltpu.roll(t, LANES - half, 1) * sin_lo


def _ada_kernel(c_ref, w_ref, b_ref, o_ref, acc_sc):
    kblk = pl.program_id(1)
    nb, kb = c_ref.shape[0], c_ref.shape[1]

    @pl.when(kblk == 0)
    def _():
        acc_sc[...] = jnp.zeros_like(acc_sc)

    cv = c_ref[...]
    s = cv * jax.nn.sigmoid(cv)
    for g in range(w_ref.shape[1] // LANES):
        cols = slice(g * LANES, (g + 1) * LANES)
        wg = w_ref[:, cols]
        for b in range(nb):
            acc_sc[b, :, cols] += jnp.sum((wg * s[b]).reshape(kb // 8, 8, LANES), axis=0)

    @pl.when(kblk == pl.num_programs(1) - 1)
    def _():
        for b in range(nb):
            o_ref[b:b + 1, :] = jnp.sum(acc_sc[b], axis=0, keepdims=True) + b_ref[...]


def _ada(c, w_ada, b_ada):
    depth, d, n = w_ada.shape
    b = c.shape[0]
    kb = _tile(d, (128, 64, 32, 16, 8))
    c_rep = jnp.broadcast_to(c[:, :, None], (b, d, LANES))
    return pl.pallas_call(
        _ada_kernel,
        name="ada",
        grid=(depth, d // kb),
        in_specs=[pl.BlockSpec((b, kb, LANES), lambda l, k: (0, k, 0)),
                  pl.BlockSpec((None, kb, n), lambda l, k: (l, k, 0)),
                  pl.BlockSpec((None, 1, n), lambda l, k: (l, 0, 0))],
        out_specs=pl.BlockSpec((None, b, n), lambda l, k: (l, 0, 0)),
        out_shape=jax.ShapeDtypeStruct((depth, b, n), F32),
        scratch_shapes=[pltpu.VMEM((b, 8, n), F32)],
        compiler_params=_cparams(("arbitrary", "arbitrary"), 2 * kb * n * 4 + 4 * b * 8 * n * 4),
    )(c_rep, w_ada, b_ada.reshape(depth, 1, n))


def _rope_table_kernel(pos_ref, freq_ref, mc_ref, m1_ref, mlo_ref, mhi_ref, cos_ref, lo_ref, hi_ref):
    ang = pos_ref[...] * freq_ref[...]
    cs = jnp.cos(ang)
    sn = jnp.sin(ang)
    cos_ref[...] = cs * mc_ref[...] + m1_ref[...]
    lo_ref[...] = sn * mlo_ref[...]
    hi_ref[...] = sn * mhi_ref[...]


def _rope_tables(pos_col, rot_dim):
    t = pos_col.shape[0]
    half = rot_dim // 2
    inv = np.exp(-math.log(ROPE_THETA) * np.arange(half, dtype=np.float32) * np.float32(2.0 / rot_dim))
    lane = np.arange(LANES)
    freq = np.where(lane < rot_dim, inv[lane % half], 0.0).astype(np.float32)[None]
    m_cos = (lane < rot_dim).astype(np.float32)[None]
    m_one = (lane >= rot_dim).astype(np.float32)[None]
    m_lo = np.where(lane < half, -1.0, 0.0).astype(np.float32)[None]
    m_hi = np.where((lane >= half) & (lane < rot_dim), 1.0, 0.0).astype(np.float32)[None]
    bm = _tile(t, (2048, 1024, 512, 256, 128))
    row = pl.BlockSpec((1, LANES), lambda i: (0, 0))
    tab = pl.BlockSpec((bm, LANES), lambda i: (i, 0))
    shp = jax.ShapeDtypeStruct((t, LANES), F32)
    return pl.pallas_call(
        _rope_table_kernel,
        name="rope_tables",
        grid=(t // bm,),
        in_specs=[pl.BlockSpec((bm, 1), lambda i: (i, 0)), row, row, row, row, row],
        out_specs=[tab, tab, tab],
        out_shape=[shp, shp, shp],
        compiler_params=_cparams(("arbitrary",), 16 * bm * LANES * 4),
    )(pos_col, jnp.asarray(freq), jnp.asarray(m_cos), jnp.asarray(m_one), jnp.asarray(m_lo), jnp.asarray(m_hi))


def _mod_kernel(x_ref, sc_ref, sh_ref, o_ref):
    o_ref[...] = (x_ref[...] * (1.0 + sc_ref[...]) + sh_ref[...]).astype(o_ref.dtype)


def _modulate(x2, sc, sh, batch):
    t, d = x2.shape
    s = t // batch
    bm = _tile(s, (512, 256, 128))
    nb = s // bm
    vec = pl.BlockSpec((None, 1, d), lambda b, i: (b, 0, 0))
    return pl.pallas_call(
        _mod_kernel,
        name="modulate",
        grid=(batch, nb),
        in_specs=[pl.BlockSpec((bm, d), lambda b, i: (b * nb + i, 0)), vec, vec],
        out_specs=pl.BlockSpec((bm, d), lambda b, i: (b * nb + i, 0)),
        out_shape=jax.ShapeDtypeStruct((t, d), BF16),
        compiler_params=_cparams(("arbitrary", "arbitrary"), 2 * bm * d * 6),
    )(x2, sc, sh)


def _res_ln_kernel(x_ref, y_ref, gt_ref, g_ref, b_ref, sc_ref, sh_ref, xo_ref, *ho_ref, alpha):
    z = alpha * x_ref[...] + (1.0 + gt_ref[...]) * y_ref[...].astype(F32)
    mu = jnp.mean(z, axis=-1, keepdims=True)
    zc = z - mu
    var = jnp.mean(zc * zc, axis=-1, keepdims=True)
    xn = zc * lax.rsqrt(var + 1e-5) * g_ref[...] + b_ref[...]
    xo_ref[...] = xn
    if ho_ref:
        ho_ref[0][...] = (xn * (1.0 + sc_ref[...]) + sh_ref[...]).astype(BF16)


def _res_ln(x2, y2, gt, ln_g, ln_b, sc, sh, batch, alpha, with_h):
    t, d = x2.shape
    s = t // batch
    bm = _tile(s, (256, 128))
    nb = s // bm
    vec = pl.BlockSpec((None, 1, d), lambda b, i: (b, 0, 0))
    par = pl.BlockSpec((1, d), lambda b, i: (0, 0))
    blk = pl.BlockSpec((bm, d), lambda b, i: (b * nb + i, 0))
    out_specs = [blk, blk] if with_h else [blk]
    out_shape = [jax.ShapeDtypeStruct((t, d), F32)]
    if with_h:
        out_shape.append(jax.ShapeDtypeStruct((t, d), BF16))
    outs = pl.pallas_call(
        functools.partial(_res_ln_kernel, alpha=alpha),
        name="res_ln",
        grid=(batch, nb),
        in_specs=[blk, blk, vec, par, par, vec, vec],
        out_specs=out_specs,
        out_shape=out_shape,
        compiler_params=_cparams(("arbitrary", "arbitrary"), 2 * bm * d * 14 + 6 * bm * d * 4),
    )(x2, y2, gt, ln_g.reshape(1, d), ln_b.reshape(1, d), sc, sh)
    return (outs[0], outs[1]) if with_h else (outs[0], None)


def _mm_kernel(x_ref, w_ref, o_ref, *, scale, n_scaled):
    acc = _dot(x_ref[...], w_ref[...])
    if n_scaled:
        acc = acc * jnp.where(pl.program_id(1) < n_scaled, scale, 1.0)
    o_ref[...] = acc.astype(o_ref.dtype)


def _matmul(x, w, out_dtype, bm_prefs=(1024, 512, 256, 128), bn_prefs=(1024, 512, 256, 128), scale=1.0,
            scaled_cols=0):
    m, k = x.shape
    n = w.shape[1]
    bm = _tile(m, bm_prefs)
    bn = _tile(n, bn_prefs)
    assert scaled_cols % bn == 0
    osz = jnp.dtype(out_dtype).itemsize
    return pl.pallas_call(
        functools.partial(_mm_kernel, scale=scale, n_scaled=scaled_cols // bn),
        name="matmul",
        grid=(m // bm, n // bn),
        in_specs=[pl.BlockSpec((bm, k), lambda i, j: (i, 0)),
                  pl.BlockSpec((k, bn), lambda i, j: (0, j))],
        out_specs=pl.BlockSpec((bm, bn), lambda i, j: (i, j)),
        out_shape=jax.ShapeDtypeStruct((m, n), out_dtype),
        compiler_params=_cparams(("arbitrary", "arbitrary"),
                                 2 * (bm * k * 2 + k * bn * 2 + bm * bn * osz) + bm * bn * 4),
    )(x, w)


def _mm_acc_kernel(x_ref, w_ref, o_ref, acc_ref):
    kk = pl.program_id(2)

    @pl.when(kk == 0)
    def _():
        acc_ref[...] = jnp.zeros_like(acc_ref)

    acc_ref[...] += _dot(x_ref[...], w_ref[...])

    @pl.when(kk == pl.num_programs(2) - 1)
    def _():
        o_ref[...] = acc_ref[...].astype(o_ref.dtype)


def _matmul_ksplit(x, w, out_dtype, nk):
    m, k = x.shape
    n = w.shape[1]
    bm = _tile(m, (1024, 512, 256, 128))
    bn = _tile(n, (512, 256, 128))
    bk = k // nk
    osz = jnp.dtype(out_dtype).itemsize
    return pl.pallas_call(
        _mm_acc_kernel,
        name="matmul_ksplit",
        grid=(m // bm, n // bn, nk),
        in_specs=[pl.BlockSpec((bm, bk), lambda i, j, q: (i, q)),
                  pl.BlockSpec((bk, bn), lambda i, j, q: (q, j))],
        out_specs=pl.BlockSpec((bm, bn), lambda i, j, q: (i, j)),
        out_shape=jax.ShapeDtypeStruct((m, n), out_dtype),
        scratch_shapes=[pltpu.VMEM((bm, bn), F32)],
        compiler_params=_cparams(("arbitrary", "arbitrary", "arbitrary"),
                                 2 * (bm * bk * 2 + bk * bn * 2 + bm * bn * osz) + 2 * bm * bn * 4),
    )(x, w)


def _proj_ac_kernel(x_ref, w_ref, cos_ref, lo_ref, hi_ref, o_ref, *, n_rot_tiles, half, scale):
    j = pl.program_id(1)
    acc = _dot(x_ref[...], w_ref[...]) * jnp.where(j < n_rot_tiles // 2, scale, 1.0)

    @pl.when(j < n_rot_tiles)
    def _():
        cs, lo, hi = cos_ref[...], lo_ref[...], hi_ref[...]
        for g in range(acc.shape[1] // LANES):
            sl = slice(g * LANES, (g + 1) * LANES)
            o_ref[:, sl] = _rotate(acc[:, sl], cs, lo, hi, half).astype(o_ref.dtype)

    @pl.when(j >= n_rot_tiles)
    def _():
        o_ref[...] = acc.astype(o_ref.dtype)


def _proj_ac(h, w_in_bf, l, scale, tabs):
    m, k = h.shape
    n = 3 * A_WIDTH
    bm = _tile(m, (1024, 512, 256, 128))
    bn = _tile(A_WIDTH, (768, 512, 256, 128))
    tab = pl.BlockSpec((bm, LANES), lambda i, j: (i, 0))
    return pl.pallas_call(
        functools.partial(_proj_ac_kernel, n_rot_tiles=2 * A_WIDTH // bn, half=PARTIAL_ROPE_DIM // 2, scale=scale),
        name="proj_ac",
        grid=(m // bm, n // bn),
        in_specs=[pl.BlockSpec((bm, k), lambda i, j: (i, 0)),
                  pl.BlockSpec((None, k, bn), lambda i, j: (l, 0, j)), tab, tab, tab],
        out_specs=pl.BlockSpec((bm, bn), lambda i, j: (i, j)),
        out_shape=jax.ShapeDtypeStruct((m, n), BF16),
        compiler_params=_cparams(("arbitrary", "arbitrary"),
                                 2 * (bm * k * 2 + k * bn * 2 + bm * bn * 2 + 3 * bm * LANES * 4) + 2 * bm * bn * 4),
    )(h, w_in_bf, *tabs)


def _rms(x, g, eps=1e-6):
    return x * lax.rsqrt(jnp.mean(x * x, axis=-1, keepdims=True) + eps) * g


def _mla_q_kernel(h_ref, wcq_ref, g_ref, wuq_ref, cos_ref, lo_ref, hi_ref, o_ref):
    cq = _dot(h_ref[...], wcq_ref[...])
    q = _dot(_rms(cq, g_ref[...]).astype(BF16), wuq_ref[...])
    cs, lo, hi = cos_ref[...], lo_ref[...], hi_ref[...]
    for hd in range(B_HEADS):
        base = hd * B_QK_PAD
        o_ref[:, base:base + QK_NOPE] = q[:, base:base + QK_NOPE].astype(BF16)
        rope = _rotate(q[:, base + QK_NOPE:base + B_QK_PAD], cs, lo, hi, QK_ROPE // 2)
        o_ref[:, base + QK_NOPE:base + B_QK_PAD] = rope.astype(BF16)


def _mla_q(h, w_in_bf, l, cq_col0, g_qn, w_uq, tabs):
    m, k = h.shape
    ql = g_qn.shape[0]
    n = w_uq.shape[1]
    assert cq_col0 % ql == 0
    bm = _tile(m, (512, 256, 128))
    tab = pl.BlockSpec((bm, LANES), lambda i: (i, 0))
    return pl.pallas_call(
        _mla_q_kernel,
        name="mla_q",
        grid=(m // bm,),
        in_specs=[pl.BlockSpec((bm, k), lambda i: (i, 0)),
                  pl.BlockSpec((None, k, ql), lambda i: (l, 0, cq_col0 // ql), pipeline_mode=pl.Buffered(1)),
                  _resident((1, ql)), _resident((ql, n)), tab, tab, tab],
        out_specs=pl.BlockSpec((bm, n), lambda i: (i, 0)),
        out_shape=jax.ShapeDtypeStruct((m, n), BF16),
        compiler_params=_cparams(("arbitrary",),
                                 2 * (bm * k * 2 + bm * n * 2) + k * ql * 2 + ql * n * 2 + bm * (ql + n) * 8),
    )(h, w_in_bf, g_qn.reshape(1, ql), w_uq, *tabs)


def _mla_kv_kernel(h_ref, wc_ref, g_ref, wukv_ref, bf_ref, cos_ref, lo_ref, hi_ref, k_ref, v_ref, f_ref, *, kvl):
    ck = _dot(h_ref[...], wc_ref[...])
    kv = _dot(_rms(ck[:, :kvl], g_ref[...]).astype(BF16), wukv_ref[...])
    kr = _rotate(ck[:, kvl:kvl + LANES], cos_ref[...], lo_ref[...], hi_ref[...], QK_ROPE // 2).astype(BF16)
    for hd in range(B_HEADS):
        base = hd * B_QK_PAD
        k_ref[:, base:base + QK_NOPE] = kv[:, hd * QK_NOPE:(hd + 1) * QK_NOPE].astype(BF16)
        k_ref[:, base + QK_NOPE:base + B_QK_PAD] = kr
    v_ref[...] = kv[:, B_HEADS * QK_NOPE:].astype(BF16)
    f_ref[...] = ck[:, kvl + LANES:] + bf_ref[...]


def _mla_kv(h, w_c, g_kvn, w_ukv, b_f_row, tabs):
    m, k = h.shape
    kvl = g_kvn.shape[0]
    nc = w_c.shape[1]
    bm = _tile(m, (512, 256, 128))
    tab = pl.BlockSpec((bm, LANES), lambda i: (i, 0))
    nk = B_HEADS * B_QK_PAD
    return pl.pallas_call(
        functools.partial(_mla_kv_kernel, kvl=kvl),
        name="mla_kv",
        grid=(m // bm,),
        in_specs=[pl.BlockSpec((bm, k), lambda i: (i, 0)),
                  _resident((k, nc)), _resident((1, kvl)), _resident(w_ukv.shape), _resident((1, LANES)),
                  tab, tab, tab],
        out_specs=[pl.BlockSpec((bm, nk), lambda i: (i, 0)),
                   pl.BlockSpec((bm, B_WIDTH), lambda i: (i, 0)),
                   pl.BlockSpec((bm, LANES), lambda i: (i, 0))],
        out_shape=[jax.ShapeDtypeStruct((m, nk), BF16),
                   jax.ShapeDtypeStruct((m, B_WIDTH), BF16),
                   jax.ShapeDtypeStruct((m, LANES), F32)],
        compiler_params=_cparams(("arbitrary",),
                                 2 * (bm * k * 2 + k * nc * 2 + w_ukv.size * 2 + bm * (nk + B_WIDTH) * 2)
                                 + bm * (nc + nk + B_WIDTH) * 8),
    )(h, w_c, g_kvn.reshape(1, kvl), w_ukv, b_f_row, *tabs)


def _fox_cumsum_kernel(f_ref, o_ref):
    x = f_ref[...]
    y = (jnp.minimum(x, 0.0) - jnp.log(1.0 + jnp.exp(-jnp.abs(x)))) * LOG2E
    s = y.shape[1]
    lane = lax.broadcasted_iota(jnp.int32, y.shape, 1)
    shift = 1
    while shift < s:
        y = y + jnp.where(lane >= shift, pltpu.roll(y, shift, 1), 0.0)
        shift *= 2
    o_ref[...] = y


def _fox_cumsum(f_t):
    b, r, s = f_t.shape
    return pl.pallas_call(
        _fox_cumsum_kernel,
        name="fox_cumsum",
        grid=(b,),
        in_specs=[pl.BlockSpec((None, r, s), lambda i: (i, 0, 0))],
        out_specs=pl.BlockSpec((None, r, s), lambda i: (i, 0, 0)),
        out_shape=jax.ShapeDtypeStruct((b, r, s), F32),
        compiler_params=_cparams(("arbitrary",), 8 * r * s * 4),
    )(f_t)


def _flash_kernel(*refs, bq, bk, has_bias):
    if has_bias:
        q_ref, k_ref, v_ref, cq_ref, ck_ref, o_ref, acc_sc, s_sc, p_sc = refs
    else:
        q_ref, k_ref, v_ref, o_ref, acc_sc, s_sc, p_sc = refs
    n_diag = bq // bk
    n_full = pl.program_id(2) * n_diag
    fused_max = not has_bias

    def causal(s, diag):
        key = lax.broadcasted_iota(jnp.int32, (bk, bq), 0) + diag * bk
        qry = lax.broadcasted_iota(jnp.int32, (bk, bq), 1)
        return jnp.where(key <= qry, s, NEG)

    def scores(j, slot, diag=None):
        start = pl.multiple_of(j * bk, bk)
        s = _dot_nt(k_ref[pl.ds(start, bk), :], q_ref[...])
        if has_bias:
            ck = ck_ref[pl.ds(start, bk), :]
            s = s + cq_ref[...] - jnp.concatenate([ck] * (bq // LANES), axis=1)
        if diag is not None:
            s = causal(s, diag)
        s_sc[slot] = s
        return jnp.max(s, axis=0, keepdims=True) if fused_max else jnp.zeros((1, bq), F32)

    def accumulate(j, alpha, slot):
        start = pl.multiple_of(j * bk, bk)
        pv = lax.dot_general(v_ref[pl.ds(start, bk), :], p_sc[slot], (((0,), (0,)), ((), ())),
                             preferred_element_type=F32)
        acc_sc[...] = alpha * acc_sc[...] + pv

    def softmax_update(m_prev, l_prev, block_max, slot):
        if not fused_max:
            block_max = jnp.max(s_sc[slot], axis=0, keepdims=True)
        m_new = jnp.maximum(m_prev, block_max)
        alpha = jnp.exp2(m_prev - m_new)
        p = jnp.exp2(s_sc[slot] - m_new)
        l_new = alpha * l_prev + jnp.sum(p, axis=0, keepdims=True)
        p_sc[slot] = p.astype(BF16)
        return m_new, l_new, alpha

    acc_sc[...] = jnp.zeros_like(acc_sc)
    p_sc[1] = jnp.zeros((bk, bq), BF16)
    bmax = scores(0, 0)

    def body(j, carry):
        m, l, alpha_prev, bmax = carry
        cur = j % 2
        accumulate(jnp.maximum(j - 1, 0), alpha_prev, 1 - cur)
        m, l, alpha = softmax_update(m, l, bmax, cur)
        return m, l, alpha, scores(j + 1, 1 - cur)

    init = (jnp.full((1, bq), NEG, F32), jnp.zeros((1, bq), F32), jnp.ones((1, bq), F32), bmax)
    m, l, alpha, bmax = lax.fori_loop(0, n_full, body, init)
    cur = n_full % 2
    s = causal(s_sc[cur], 0)
    s_sc[cur] = s
    bmax = jnp.max(s, axis=0, keepdims=True)
    for t in range(n_diag):
        j = n_full + t
        cur = j % 2
        accumulate(jnp.maximum(j - 1, 0), alpha, 1 - cur)
        m, l, alpha = softmax_update(m, l, bmax, cur)
        if t + 1 < n_diag:
            bmax = scores(j + 1, 1 - cur, t + 1)
    accumulate(n_full + n_diag - 1, alpha, cur)
    o_ref[...] = jnp.transpose(acc_sc[...] / l).astype(o_ref.dtype)


def _flash(q_arr, k_arr, v_arr, q_col0, k_col0, v_col0, dq, dv, heads, batch, seq, bias=None):
    bq = _tile(seq, (FLASH_BQ, 512, 256, 128))
    bk = _tile(bq, (FLASH_BK, 256, 128))
    nq = seq // bq
    t = batch * seq
    in_specs = [pl.BlockSpec((bq, dq), lambda b, h, i: (b * nq + i, q_col0 + h)),
                pl.BlockSpec((seq, dq), lambda b, h, i: (b, k_col0 + h)),
                pl.BlockSpec((seq, dv), lambda b, h, i: (b, v_col0 + h))]
    args = [q_arr, k_arr, v_arr]
    if bias is not None:
        c_row, c_rep = bias
        in_specs += [pl.BlockSpec((None, 1, bq), lambda b, h, i: (b * F_ROWS + h, 0, i)),
                     pl.BlockSpec((None, seq, LANES), lambda b, h, i: (b * heads + h, 0, 0))]
        args += [c_row, c_rep]
    return pl.pallas_call(
        functools.partial(_flash_kernel, bq=bq, bk=bk, has_bias=bias is not None),
        name="flash_fox" if bias is not None else "flash_mla",
        grid=(batch, heads, nq),
        in_specs=in_specs,
        out_specs=pl.BlockSpec((bq, dv), lambda b, h, i: (b * nq + i, h)),
        out_shape=jax.ShapeDtypeStruct((t, heads * dv), BF16),
        scratch_shapes=[pltpu.VMEM((dv, bq), F32), pltpu.VMEM((2, bk, bq), F32), pltpu.VMEM((2, bk, bq), BF16)],
        compiler_params=_cparams(("arbitrary", "arbitrary", "arbitrary"),
                                 2 * (seq * (dq + dv) * 2 + bq * (dq + dv) * 2 + seq * LANES * 4)
                                 + 8 * bq * bk * 4),
    )(*args)


def _dilated_kernel(q_ref, kp_ref, kc_ref, vp_ref, vc_ref, o_ref, lse_ref, kband, vband, *, rows, blk):
    n = pl.program_id(2)
    kband[0:blk, :] = kp_ref[...]
    kband[blk:, :] = kc_ref[...]
    vband[0:blk, :] = vp_ref[...]
    vband[blk:, :] = vc_ref[...]
    qi = lax.broadcasted_iota(jnp.int32, (blk, 2 * blk), 0)
    ki = lax.broadcasted_iota(jnp.int32, (blk, 2 * blk), 1)
    window = (ki >= qi) & (ki <= qi + blk)
    lane = lax.broadcasted_iota(jnp.int32, (blk, LANES), 1)

    def sub_block(a, carry):
        ro = pl.multiple_of(a * blk, blk)
        first_key = jnp.where(n * rows + ro > 0, 0, blk)
        mask = window & (ki >= first_key)
        lse_tile = jnp.zeros((blk, LANES), F32)
        for hd in range(A_HEADS):
            cs = slice(hd * HEAD_DIM, (hd + 1) * HEAD_DIM)
            s = _dot_nt(q_ref[pl.ds(ro, blk), cs], kband[pl.ds(ro, 2 * blk), cs])
            s = jnp.where(mask, s, NEG)
            m = jnp.max(s, axis=-1, keepdims=True)
            p = jnp.exp2(s - m)
            l = jnp.sum(p, axis=-1, keepdims=True)
            o = _dot(p.astype(BF16), vband[pl.ds(ro, 2 * blk), cs]) / l
            o_ref[pl.ds(ro, blk), cs] = o.astype(o_ref.dtype)
            lse_tile = jnp.where(lane == hd, m + jnp.log2(l), lse_tile)
        lse_ref[pl.ds(ro, blk), :] = lse_tile
        return carry

    lax.fori_loop(0, rows // blk, sub_block, 0)


def _dilated_group(qkv, row_width, batch, seq, window, dilation):
    blk = window // dilation
    sub = seq // dilation
    assert sub % blk == 0 and row_width % A_WIDTH == 0
    rows = _tile(sub, (4 * blk, 2 * blk, blk))
    nb = sub // rows
    rpb = rows // blk
    cpr = row_width // A_WIDTH
    t = batch * seq
    view = qkv.reshape(t // dilation, dilation * row_width)

    def cur(c):
        return pl.BlockSpec((rows, A_WIDTH), lambda b, r, n: (b * nb + n, cpr * r + c))

    def prev(c):
        return pl.BlockSpec((blk, A_WIDTH),
                            lambda b, r, n: (b * (sub // blk) + jnp.maximum(n * rpb - 1, 0), cpr * r + c))

    o, lse = pl.pallas_call(
        functools.partial(_dilated_kernel, rows=rows, blk=blk),
        name=f"dilated_d{dilation}",
        grid=(batch, dilation, nb),
        in_specs=[cur(0), prev(1), cur(1), prev(2), cur(2)],
        out_specs=[pl.BlockSpec((rows, A_WIDTH), lambda b, r, n: (b * nb + n, r)),
                   pl.BlockSpec((rows, LANES), lambda b, r, n: (b * nb + n, r))],
        out_shape=[jax.ShapeDtypeStruct((t // dilation, dilation * A_WIDTH), BF16),
                   jax.ShapeDtypeStruct((t // dilation, dilation * LANES), F32)],
        scratch_shapes=[pltpu.VMEM((rows + blk, A_WIDTH), BF16), pltpu.VMEM((rows + blk, A_WIDTH), BF16)],
        compiler_params=_cparams(("arbitrary", "arbitrary", "arbitrary"),
                                 2 * (3 * rows + 2 * blk) * A_WIDTH * 2 + 2 * rows * (A_WIDTH + LANES) * 4
                                 + 2 * (rows + blk) * A_WIDTH * 2 + 16 * blk * 2 * blk * 4),
    )(view, view, view, view, view)
    return o.reshape(t, A_WIDTH), lse.reshape(t, LANES)


DEINTERLEAVE_ROWS = 16 * max(d for _, d in A_PATTERNS)


def _deinterleave_kernel(x_ref, *refs, dilations):
    out_refs, scr = refs[:-1], refs[-1]
    rows = scr.shape[1]
    for g in range(scr.shape[0]):
        cols = slice(g * LANES, (g + 1) * LANES)
        scr[g] = x_ref[:, cols].astype(F32)
        for o_ref, d in zip(out_refs, dilations):
            for r in range(d):
                o_ref[r, :, cols] = scr[g, pl.ds(r, rows // d, stride=d), :].astype(o_ref.dtype)


def _deinterleave(x, batch, seq, dilations):
    t, w = x.shape
    rows = DEINTERLEAVE_ROWS
    nb = seq // rows
    outs = pl.pallas_call(
        functools.partial(_deinterleave_kernel, dilations=dilations),
        name="dilated_deinterleave",
        grid=(batch, nb),
        in_specs=[pl.BlockSpec((rows, w), lambda b, n: (b * nb + n, 0))],
        out_specs=[pl.BlockSpec((None, d, rows // d, w), lambda b, n: (b, 0, n, 0)) for d in dilations],
        out_shape=[jax.ShapeDtypeStruct((batch, d, seq // d, w), x.dtype) for d in dilations],
        scratch_shapes=[pltpu.VMEM((w // LANES, rows, LANES), F32)],
        compiler_params=_cparams(("arbitrary", "arbitrary"), rows * w * (4 + 4 + 4 * len(dilations) + 8)),
    )(x)
    return [o.reshape(t, w) for o in outs]


def _combine_kernel(*refs, dilations):
    n = len(dilations)
    o_refs, l_refs, out_ref, scratch = refs[:n], refs[n:2 * n], refs[2 * n], refs[2 * n + 1:]
    heads, lses = [], []
    for o_ref, l_ref, d in zip(o_refs, l_refs, dilations):
        if d == 1:
            heads.append(lambda hd, o_ref=o_ref: o_ref[:, hd * HEAD_DIM:(hd + 1) * HEAD_DIM])
            lses.append(l_ref[...])
            continue
        o_sc, l_sc = scratch[:2]
        scratch = scratch[2:]
        rows = o_sc.shape[1]
        for r in range(d):
            dst = pl.ds(r, rows // d, stride=d)
            l_sc[0, dst, :] = l_ref[r]
            for hd in range(A_HEADS):
                o_sc[hd, dst, :] = o_ref[r, :, hd * HEAD_DIM:(hd + 1) * HEAD_DIM].astype(F32)
        heads.append(lambda hd, o_sc=o_sc: o_sc[hd])
        lses.append(l_sc[0])
    mx = functools.reduce(jnp.maximum, lses)
    es = [jnp.exp2(a - mx) for a in lses]
    inv = 1.0 / functools.reduce(jnp.add, es)
    ws = [e * inv for e in es]
    for hd in range(A_HEADS):
        acc = functools.reduce(jnp.add, [w[:, hd:hd + 1] * head(hd) for w, head in zip(ws, heads)])
        out_ref[:, hd * HEAD_DIM:(hd + 1) * HEAD_DIM] = acc.astype(out_ref.dtype)


def _combine(outs, lses, batch, seq, dilations):
    t = batch * seq
    rows = DEINTERLEAVE_ROWS
    nb = seq // rows
    in_specs, args, scratch = [], [], []
    for width, arrs in ((A_WIDTH, outs), (LANES, lses)):
        for a, d in zip(arrs, dilations):
            if d == 1:
                in_specs.append(pl.BlockSpec((rows, width), lambda b, n: (b * nb + n, 0)))
                args.append(a)
            else:
                in_specs.append(pl.BlockSpec((None, d, rows // d, width), lambda b, n: (b, 0, n, 0)))
                args.append(a.reshape(batch, d, seq // d, width))
    for d in dilations:
        if d != 1:
            scratch += [pltpu.VMEM((A_HEADS, rows, LANES), F32), pltpu.VMEM((1, rows, LANES), F32)]
    return pl.pallas_call(
        functools.partial(_combine_kernel, dilations=dilations),
        name="dilated_combine",
        grid=(batch, nb),
        in_specs=in_specs,
        out_specs=pl.BlockSpec((rows, A_WIDTH), lambda b, n: (b * nb + n, 0)),
        out_shape=jax.ShapeDtypeStruct((t, A_WIDTH), BF16),
        scratch_shapes=scratch,
        compiler_params=_cparams(("arbitrary", "arbitrary"),
                                 rows * (A_WIDTH + LANES) * 4 * (3 * len(dilations) + 4)),
    )(*args)


def _merge_kernel(h_ref, oa_ref, ob_ref, oc_ref, wg0, wg1, wg2, wa, wb, wc, o_ref):
    h = h_ref[...]
    acc = jax.nn.sigmoid(_dot(h, wg0[...])) * _dot(oa_ref[...], wa[...])
    acc += jax.nn.sigmoid(_dot(h, wg1[...])) * _dot(ob_ref[...], wb[...])
    acc += jax.nn.sigmoid(_dot(h, wg2[...])) * _dot(oc_ref[...], wc[...])
    o_ref[...] = acc.astype(o_ref.dtype)


def _merge(h, oa, ob, oc, w_gate, w_a, w_b, w_c):
    m, d = h.shape
    bm = _tile(m, (512, 256, 128))
    bn = _tile(d, (512, 256, 128))
    nj = d // bn

    def rows(width):
        return pl.BlockSpec((bm, width), lambda i, j: (i, 0))

    def gate(g):
        return pl.BlockSpec((d, bn), lambda i, j: (0, g * nj + j))

    def branch(width):
        return pl.BlockSpec((width, bn), lambda i, j: (0, j))

    k_all = 3 * d + A_WIDTH + B_WIDTH + C_WIDTH
    return pl.pallas_call(
        _merge_kernel,
        name="gate_merge",
        grid=(m // bm, nj),
        in_specs=[rows(d), rows(A_WIDTH), rows(B_WIDTH), rows(C_WIDTH), gate(0), gate(1), gate(2),
                  branch(A_WIDTH), branch(B_WIDTH), branch(C_WIDTH)],
        out_specs=pl.BlockSpec((bm, bn), lambda i, j: (i, j)),
        out_shape=jax.ShapeDtypeStruct((m, d), BF16),
        compiler_params=_cparams(("arbitrary", "arbitrary"),
                                 2 * (bm * (d + A_WIDTH + B_WIDTH + C_WIDTH) * 2 + k_all * bn * 2 + bm * bn * 2)
                                 + 8 * bm * bn * 4),
    )(h, oa, ob, oc, w_gate, w_gate, w_gate, w_a, w_b, w_c)


def _ffn_in_kernel(h_ref, wa_ref, wb_ref, o_ref):
    h = h_ref[...]
    a = _dot(h, wa_ref[...])
    b = _dot(h, wb_ref[...])
    o_ref[...] = (a * jax.nn.sigmoid(a) * b).astype(o_ref.dtype)


def _ffn_in(h, w_in, d_ff):
    m, d = h.shape
    bm = _tile(m, (1024, 512, 256, 128))
    bn = _tile(d_ff, (512, 256, 128))
    nj = d_ff // bn
    return pl.pallas_call(
        _ffn_in_kernel,
        name="ffn_in",
        grid=(m // bm, nj),
        in_specs=[pl.BlockSpec((bm, d), lambda i, j: (i, 0)),
                  pl.BlockSpec((d, bn), lambda i, j: (0, j)),
                  pl.BlockSpec((d, bn), lambda i, j: (0, nj + j))],
        out_specs=pl.BlockSpec((bm, bn), lambda i, j: (i, j)),
        out_shape=jax.ShapeDtypeStruct((m, d_ff), BF16),
        compiler_params=_cparams(("arbitrary", "arbitrary"),
                                 2 * (bm * d * 2 + 2 * d * bn * 2 + bm * bn * 2) + 4 * bm * bn * 4),
    )(h, w_in, w_in)


def _layer(x2, h, ada_l, ada_next, batch, seq, tabs_a, tabs_b, w_in_bf, l, w, alpha, last):
    d = x2.shape[1]
    t = batch * seq
    sh1, sc1, gt1, sh2, sc2, gt2 = [a.reshape(batch, 1, d) for a in jnp.split(ada_l, 6, axis=-1)]
    del sh1, sc1

    qkv_a = _proj_ac(h, w_in_bf, l, SCALE_A, tabs_a)
    qkv_c = _matmul(h, w["c"], BF16, bn_prefs=(768, 512, 256, 128), scale=SCALE_A, scaled_cols=C_WIDTH)
    q_b = _mla_q(h, w_in_bf, l, w["cq_col0"], w["g_qn"], w["uq"], tabs_b)
    k_b, v_b, f_logit = _mla_kv(h, w["ckv"], w["g_kvn"], w["ukv"], w["b_f"], tabs_b)

    dilations = tuple(dil for _, dil in A_PATTERNS)
    strided = [dil for dil in dilations if dil != 1]
    copies = dict(zip(strided, _deinterleave(qkv_a, batch, seq, strided)))
    outs, lses = [], []
    for window, dil in A_PATTERNS:
        src = qkv_a if dil == 1 else copies[dil]
        o, l = _dilated_group(src, src.shape[1], batch * dil, seq // dil, window // dil, 1)
        outs.append(o)
        lses.append(l)
    o_a = _combine(outs, lses, batch, seq, dilations)

    o_b = _flash(q_b, k_b, v_b, 0, 0, 0, B_QK_PAD, V_DIM, B_HEADS, batch, seq)

    f_t = f_logit[:, :F_ROWS].reshape(batch, seq, F_ROWS).transpose(0, 2, 1)
    c_t = _fox_cumsum(f_t)
    c_row = c_t.reshape(batch * F_ROWS, 1, seq)
    c_rep = jnp.broadcast_to(c_t[:, :C_HEADS, :, None], (batch, C_HEADS, seq, LANES))
    c_rep = c_rep.reshape(batch * C_HEADS, seq, LANES)
    nh = A_WIDTH // HEAD_DIM
    o_c = _flash(qkv_c, qkv_c, qkv_c, 0, nh, 2 * nh, HEAD_DIM, HEAD_DIM, C_HEADS, batch, seq,
                 bias=(c_row, c_rep))

    merged = _merge(h, o_a, o_b, o_c, w["gate"], w["br_a"], w["br_b"], w["br_c"])
    y = _matmul(merged, w["o"], BF16)
    x2, h2 = _res_ln(x2, y, gt1, w["ln1_g"], w["ln1_b"], sc2, sh2, batch, alpha, True)

    act = _ffn_in(h2, w["ffn_in"], w["d_ff"])
    nk = 2 if (w["d_ff"] // 2) % LANES == 0 else 1
    y = _matmul_ksplit(act, w["ffn_out"], BF16, nk)
    if last:
        x2, hn = _res_ln(x2, y, gt2, w["ln2_g"], w["ln2_b"], sc2, sh2, batch, alpha, False)
    else:
        sh1n, sc1n = [a.reshape(batch, 1, d) for a in jnp.split(ada_next, 6, axis=-1)[:2]]
        x2, hn = _res_ln(x2, y, gt2, w["ln2_g"], w["ln2_b"], sc1n, sh1n, batch, alpha, True)
    return x2, hn


SCALE_A = HEAD_DIM ** -0.5 * LOG2E
SCALE_B = (QK_NOPE + QK_ROPE) ** -0.5 * LOG2E


def _prep_weights(l, w_in_bf, b_f, g_qn, w_uq, g_kvn, w_ukv, w_br_a, w_br_b, w_br_c, w_o,
                  ln1_g, ln1_b, w_ffn_in, w_ffn_out, ln2_g, ln2_b):
    d = w_in_bf.shape[1]
    ql = g_qn.shape[1]
    kvl = g_kvn.shape[1]
    wi = w_in_bf[l]
    widths = (A_WIDTH, A_WIDTH, A_WIDTH, ql, kvl, QK_ROPE, C_WIDTH, C_WIDTH, C_WIDTH, C_HEADS, N_BRANCH * d)
    offs = [int(o) for o in np.concatenate([[0], np.cumsum(widths)])]
    ckv, kr, fl = [wi[:, offs[i]:offs[i + 1]] for i in (4, 5, 9)]
    zpad = lambda n: jnp.zeros((d, n), BF16)
    w_ckv = jnp.concatenate([ckv, kr, zpad(LANES - QK_ROPE), fl, zpad(LANES - C_HEADS)], axis=1)
    uq = w_uq[l].reshape(ql, B_HEADS, QK_NOPE + QK_ROPE) * SCALE_B
    uq = jnp.pad(uq, ((0, 0), (0, 0), (0, B_QK_PAD - QK_NOPE - QK_ROPE))).reshape(ql, B_HEADS * B_QK_PAD)
    ukv = w_ukv[l].reshape(kvl, B_HEADS, 2, QK_NOPE).transpose(0, 2, 1, 3).reshape(kvl, 2 * B_HEADS * QK_NOPE)
    b_f_row = jnp.pad(b_f[l], (0, LANES - C_HEADS)).reshape(1, LANES)
    return dict(cq_col0=offs[3], c=wi[:, offs[6]:offs[9]], g_qn=g_qn[l], uq=uq.astype(BF16), ckv=w_ckv,
                g_kvn=g_kvn[l], ukv=ukv.astype(BF16), b_f=b_f_row, gate=wi[:, offs[10]:],
                br_a=w_br_a[l].astype(BF16), br_b=w_br_b[l].astype(BF16), br_c=w_br_c[l].astype(BF16),
                o=w_o[l].astype(BF16), ln1_g=ln1_g[l], ln1_b=ln1_b[l], ffn_in=w_ffn_in[l].astype(BF16),
                ffn_out=w_ffn_out[l].astype(BF16), d_ff=w_ffn_out.shape[1], ln2_g=ln2_g[l], ln2_b=ln2_b[l])


def kernel(x, c, positions, w_ada, b_ada, w_in, b_f, g_qn, w_uq, g_kvn, w_ukv, w_br_a, w_br_b, w_br_c, w_o,
           ln1_g, ln1_b, w_ffn_in, w_ffn_out, ln2_g, ln2_b):
    batch, seq, d = x.shape
    depth = w_ada.shape[0]
    alpha = (2.0 * depth) ** 0.25
    t = batch * seq
    x2 = x.reshape(t, d)
    ada = _ada(c, w_ada, b_ada)
    pos_col = positions.astype(F32).reshape(t, 1)
    tabs_a = _rope_tables(pos_col, PARTIAL_ROPE_DIM)
    tabs_b = _rope_tables(pos_col, QK_ROPE)
    sh1, sc1 = [a.reshape(batch, 1, d) for a in jnp.split(ada[0], 6, axis=-1)[:2]]
    h = _modulate(x2, sc1, sh1, batch)
    w_in_bf = w_in.astype(BF16)
    for l in range(depth):
        w = _prep_weights(l, w_in_bf, b_f, g_qn, w_uq, g_kvn, w_ukv, w_br_a, w_br_b, w_br_c, w_o,
                          ln1_g, ln1_b, w_ffn_in, w_ffn_out, ln2_g, ln2_b)
        last = l == depth - 1
        x2, h = _layer(x2, h, ada[l], None if last else ada[l + 1], batch, seq, tabs_a, tabs_b, w_in_bf, l, w,
                       alpha, last)
    return x2.reshape(batch, seq, d)
```

```python
import functools
import math

import jax
import jax.numpy as jnp
import numpy as np
from jax import lax
from jax.experimental import pallas as pl
from jax.experimental.pallas import tpu as pltpu

HEAD_DIM = 128
ROPE_THETA = 500000.0
PARTIAL_ROPE_DIM = HEAD_DIM // 4
A_HEADS = 12
A_PATTERNS = ((128, 1), (512, 4), (2048, 16))
B_HEADS = 8
QK_NOPE = 128
QK_ROPE = 64
V_DIM = 128
C_HEADS = 12
N_BRANCH = 3
A_WIDTH = A_HEADS * HEAD_DIM
B_WIDTH = B_HEADS * V_DIM
C_WIDTH = C_HEADS * HEAD_DIM
B_QK_PAD = 256
F_ROWS = 16
FLASH_BQ = 4096
FLASH_BK = 1024
NEG = -1e30
LOG2E = math.log2(math.e)
LANES = 128
V7X_VMEM_CAP = 60 * 1024 * 1024

BF16 = jnp.bfloat16
F32 = jnp.float32


def _cparams(semantics, vmem_estimate):
    limit = int(min(max(vmem_estimate * 5 // 4, 32 * 1024 * 1024), V7X_VMEM_CAP))
    return pltpu.CompilerParams(dimension_semantics=semantics, vmem_limit_bytes=limit)


def _tile(n, prefs):
    for p in prefs:
        if n % p == 0:
            return p
    return n


def _resident(shape):
    return pl.BlockSpec(shape, lambda i: (0,) * len(shape), pipeline_mode=pl.Buffered(1))


def _dot(a, b):
    return jnp.dot(a, b, preferred_element_type=F32)


def _dot_nt(a, b):
    return lax.dot_general(a, b, (((1,), (1,)), ((), ())), preferred_element_type=F32)


def _rotate(t, cos, sin_lo, sin_hi, half):
    return t * cos + pltpu.roll(t, half, 1) * sin_hi + pltpu.roll(t, LANES - half, 1) * sin_lo


def _ada_kernel(c_ref, w_ref, b_ref, o_ref, acc_sc):
    kblk = pl.program_id(1)
    nb, kb = c_ref.shape[0], c_ref.shape[1]

    @pl.when(kblk == 0)
    def _():
        acc_sc[...] = jnp.zeros_like(acc_sc)

    cv = c_ref[...]
    s = cv * jax.nn.sigmoid(cv)
    for g in range(w_ref.shape[1] // LANES):
        cols = slice(g * LANES, (g + 1) * LANES)
        wg = w_ref[:, cols]
        for b in range(nb):
            acc_sc[b, :, cols] += jnp.sum((wg * s[b]).reshape(kb // 8, 8, LANES), axis=0)

    @pl.when(kblk == pl.num_programs(1) - 1)
    def _():
        for b in range(nb):
            o_ref[b:b + 1, :] = jnp.sum(acc_sc[b], axis=0, keepdims=True) + b_ref[...]


def _ada(c, w_ada, b_ada):
    depth, d, n = w_ada.shape
    b = c.shape[0]
    kb = _tile(d, (128, 64, 32, 16, 8))
    c_rep = jnp.broadcast_to(c[:, :, None], (b, d, LANES))
    return pl.pallas_call(
        _ada_kernel,
        name="ada",
        grid=(depth, d // kb),
        in_specs=[pl.BlockSpec((b, kb, LANES), lambda l, k: (0, k, 0)),
                  pl.BlockSpec((None, kb, n), lambda l, k: (l, k, 0)),
                  pl.BlockSpec((None, 1, n), lambda l, k: (l, 0, 0))],
        out_specs=pl.BlockSpec((None, b, n), lambda l, k: (l, 0, 0)),
        out_shape=jax.ShapeDtypeStruct((depth, b, n), F32),
        scratch_shapes=[pltpu.VMEM((b, 8, n), F32)],
        compiler_params=_cparams(("arbitrary", "arbitrary"), 2 * kb * n * 4 + 4 * b * 8 * n * 4),
    )(c_rep, w_ada, b_ada.reshape(depth, 1, n))


def _rope_table_kernel(pos_ref, freq_ref, mc_ref, m1_ref, mlo_ref, mhi_ref, cos_ref, lo_ref, hi_ref):
    ang = pos_ref[...] * freq_ref[...]
    cs = jnp.cos(ang)
    sn = jnp.sin(ang)
    cos_ref[...] = cs * mc_ref[...] + m1_ref[...]
    lo_ref[...] = sn * mlo_ref[...]
    hi_ref[...] = sn * mhi_ref[...]


def _rope_tables(pos_col, rot_dim):
    t = pos_col.shape[0]
    half = rot_dim // 2
    inv = np.exp(-math.log(ROPE_THETA) * np.arange(half, dtype=np.float32) * np.float32(2.0 / rot_dim))
    lane = np.arange(LANES)
    freq = np.where(lane < rot_dim, inv[lane % half], 0.0).astype(np.float32)[None]
    m_cos = (lane < rot_dim).astype(np.float32)[None]
    m_one = (lane >= rot_dim).astype(np.float32)[None]
    m_lo = np.where(lane < half, -1.0, 0.0).astype(np.float32)[None]
    m_hi = np.where((lane >= half) & (lane < rot_dim), 1.0, 0.0).astype(np.float32)[None]
    bm = _tile(t, (2048, 1024, 512, 256, 128))
    row = pl.BlockSpec((1, LANES), lambda i: (0, 0))
    tab = pl.BlockSpec((bm, LANES), lambda i: (i, 0))
    shp = jax.ShapeDtypeStruct((t, LANES), F32)
    return pl.pallas_call(
        _rope_table_kernel,
        name="rope_tables",
        grid=(t // bm,),
        in_specs=[pl.BlockSpec((bm, 1), lambda i: (i, 0)), row, row, row, row, row],
        out_specs=[tab, tab, tab],
        out_shape=[shp, shp, shp],
        compiler_params=_cparams(("arbitrary",), 16 * bm * LANES * 4),
    )(pos_col, jnp.asarray(freq), jnp.asarray(m_cos), jnp.asarray(m_one), jnp.asarray(m_lo), jnp.asarray(m_hi))


def _mod_kernel(x_ref, sc_ref, sh_ref, o_ref):
    o_ref[...] = (x_ref[...] * (1.0 + sc_ref[...]) + sh_ref[...]).astype(o_ref.dtype)


def _modulate(x2, sc, sh, batch):
    t, d = x2.shape
    s = t // batch
    bm = _tile(s, (512, 256, 128))
    nb = s // bm
    vec = pl.BlockSpec((None, 1, d), lambda b, i: (b, 0, 0))
    return pl.pallas_call(
        _mod_kernel,
        name="modulate",
        grid=(batch, nb),
        in_specs=[pl.BlockSpec((bm, d), lambda b, i: (b * nb + i, 0)), vec, vec],
        out_specs=pl.BlockSpec((bm, d), lambda b, i: (b * nb + i, 0)),
        out_shape=jax.ShapeDtypeStruct((t, d), BF16),
        compiler_params=_cparams(("arbitrary", "arbitrary"), 2 * bm * d * 6),
    )(x2, sc, sh)


def _res_ln_kernel(x_ref, y_ref, gt_ref, g_ref, b_ref, sc_ref, sh_ref, xo_ref, *ho_ref, alpha):
    z = alpha * x_ref[...] + (1.0 + gt_ref[...]) * y_ref[...].astype(F32)
    mu = jnp.mean(z, axis=-1, keepdims=True)
    zc = z - mu
    var = jnp.mean(zc * zc, axis=-1, keepdims=True)
    xn = zc * lax.rsqrt(var + 1e-5) * g_ref[...] + b_ref[...]
    xo_ref[...] = xn
    if ho_ref:
        ho_ref[0][...] = (xn * (1.0 + sc_ref[...]) + sh_ref[...]).astype(BF16)


def _res_ln(x2, y2, gt, ln_g, ln_b, sc, sh, batch, alpha, with_h):
    t, d = x2.shape
    s = t // batch
    bm = _tile(s, (256, 128))
    nb = s // bm
    vec = pl.BlockSpec((None, 1, d), lambda b, i: (b, 0, 0))
    par = pl.BlockSpec((1, d), lambda b, i: (0, 0))
    blk = pl.BlockSpec((bm, d), lambda b, i: (b * nb + i, 0))
    out_specs = [blk, blk] if with_h else [blk]
    out_shape = [jax.ShapeDtypeStruct((t, d), F32)]
    if with_h:
        out_shape.append(jax.ShapeDtypeStruct((t, d), BF16))
    outs = pl.pallas_call(
        functools.partial(_res_ln_kernel, alpha=alpha),
        name="res_ln",
        grid=(batch, nb),
        in_specs=[blk, blk, vec, par, par, vec, vec],
        out_specs=out_specs,
        out_shape=out_shape,
        compiler_params=_cparams(("arbitrary", "arbitrary"), 2 * bm * d * 14 + 6 * bm * d * 4),
    )(x2, y2, gt, ln_g.reshape(1, d), ln_b.reshape(1, d), sc, sh)
    return (outs[0], outs[1]) if with_h else (outs[0], None)


def _mm_kernel(x_ref, w_ref, o_ref, *, scale, n_scaled):
    acc = _dot(x_ref[...], w_ref[...])
    if n_scaled:
        acc = acc * jnp.where(pl.program_id(1) < n_scaled, scale, 1.0)
    o_ref[...] = acc.astype(o_ref.dtype)


def _matmul(x, w, out_dtype, bm_prefs=(1024, 512, 256, 128), bn_prefs=(1024, 512, 256, 128), scale=1.0,
            scaled_cols=0):
    m, k = x.shape
    n = w.shape[1]
    bm = _tile(m, bm_prefs)
    bn = _tile(n, bn_prefs)
    assert scaled_cols % bn == 0
    osz = jnp.dtype(out_dtype).itemsize
    return pl.pallas_call(
        functools.partial(_mm_kernel, scale=scale, n_scaled=scaled_cols // bn),
        name="matmul",
        grid=(m // bm, n // bn),
        in_specs=[pl.BlockSpec((bm, k), lambda i, j: (i, 0)),
                  pl.BlockSpec((k, bn), lambda i, j: (0, j))],
        out_specs=pl.BlockSpec((bm, bn), lambda i, j: (i, j)),
        out_shape=jax.ShapeDtypeStruct((m, n), out_dtype),
        compiler_params=_cparams(("arbitrary", "arbitrary"),
                                 2 * (bm * k * 2 + k * bn * 2 + bm * bn * osz) + bm * bn * 4),
    )(x, w)


def _mm_acc_kernel(x_ref, w_ref, o_ref, acc_ref):
    kk = pl.program_id(2)

    @pl.when(kk == 0)
    def _():
        acc_ref[...] = jnp.zeros_like(acc_ref)

    acc_ref[...] += _dot(x_ref[...], w_ref[...])

    @pl.when(kk == pl.num_programs(2) - 1)
    def _():
        o_ref[...] = acc_ref[...].astype(o_ref.dtype)


def _matmul_ksplit(x, w, out_dtype, nk):
    m, k = x.shape
    n = w.shape[1]
    bm = _tile(m, (1024, 512, 256, 128))
    bn = _tile(n, (512, 256, 128))
    bk = k // nk
    osz = jnp.dtype(out_dtype).itemsize
    return pl.pallas_call(
        _mm_acc_kernel,
        name="matmul_ksplit",
        grid=(m // bm, n // bn, nk),
        in_specs=[pl.BlockSpec((bm, bk), lambda i, j, q: (i, q)),
                  pl.BlockSpec((bk, bn), lambda i, j, q: (q, j))],
        out_specs=pl.BlockSpec((bm, bn), lambda i, j, q: (i, j)),
        out_shape=jax.ShapeDtypeStruct((m, n), out_dtype),
        scratch_shapes=[pltpu.VMEM((bm, bn), F32)],
        compiler_params=_cparams(("arbitrary", "arbitrary", "arbitrary"),
                                 2 * (bm * bk * 2 + bk * bn * 2 + bm * bn * osz) + 2 * bm * bn * 4),
    )(x, w)


def _proj_ac_kernel(x_ref, w_ref, cos_ref, lo_ref, hi_ref, o_ref, *, n_rot_tiles, half, scale):
    j = pl.program_id(1)
    acc = _dot(x_ref[...], w_ref[...]) * jnp.where(j < n_rot_tiles // 2, scale, 1.0)

    @pl.when(j < n_rot_tiles)
    def _():
        cs, lo, hi = cos_ref[...], lo_ref[...], hi_ref[...]
        for g in range(acc.shape[1] // LANES):
            sl = slice(g * LANES, (g + 1) * LANES)
            o_ref[:, sl] = _rotate(acc[:, sl], cs, lo, hi, half).astype(o_ref.dtype)

    @pl.when(j >= n_rot_tiles)
    def _():
        o_ref[...] = acc.astype(o_ref.dtype)


def _proj_ac(h, w_in_bf, l, scale, tabs):
    m, k = h.shape
    n = 3 * A_WIDTH
    bm = _tile(m, (1024, 512, 256, 128))
    bn = _tile(A_WIDTH, (768, 512, 256, 128))
    tab = pl.BlockSpec((bm, LANES), lambda i, j: (i, 0))
    return pl.pallas_call(
        functools.partial(_proj_ac_kernel, n_rot_tiles=2 * A_WIDTH // bn, half=PARTIAL_ROPE_DIM // 2, scale=scale),
        name="proj_ac",
        grid=(m // bm, n // bn),
        in_specs=[pl.BlockSpec((bm, k), lambda i, j: (i, 0)),
                  pl.BlockSpec((None, k, bn), lambda i, j: (l, 0, j)), tab, tab, tab],
        out_specs=pl.BlockSpec((bm, bn), lambda i, j: (i, j)),
        out_shape=jax.ShapeDtypeStruct((m, n), BF16),
        compiler_params=_cparams(("arbitrary", "arbitrary"),
                                 2 * (bm * k * 2 + k * bn * 2 + bm * bn * 2 + 3 * bm * LANES * 4) + 2 * bm * bn * 4),
    )(h, w_in_bf, *tabs)


def _rms(x, g, eps=1e-6):
    return x * lax.rsqrt(jnp.mean(x * x, axis=-1, keepdims=True) + eps) * g


def _mla_q_kernel(h_ref, wcq_ref, g_ref, wuq_ref, cos_ref, lo_ref, hi_ref, o_ref):
    cq = _dot(h_ref[...], wcq_ref[...])
    q = _dot(_rms(cq, g_ref[...]).astype(BF16), wuq_ref[...])
    cs, lo, hi = cos_ref[...], lo_ref[...], hi_ref[...]
    for hd in range(B_HEADS):
        base = hd * B_QK_PAD
        o_ref[:, base:base + QK_NOPE] = q[:, base:base + QK_NOPE].astype(BF16)
        rope = _rotate(q[:, base + QK_NOPE:base + B_QK_PAD], cs, lo, hi, QK_ROPE // 2)
        o_ref[:, base + QK_NOPE:base + B_QK_PAD] = rope.astype(BF16)


def _mla_q(h, w_in_bf, l, cq_col0, g_qn, w_uq, tabs):
    m, k = h.shape
    ql = g_qn.shape[0]
    n = w_uq.shape[1]
    assert cq_col0 % ql == 0
    bm = _tile(m, (512, 256, 128))
    tab = pl.BlockSpec((bm, LANES), lambda i: (i, 0))
    return pl.pallas_call(
        _mla_q_kernel,
        name="mla_q",
        grid=(m // bm,),
        in_specs=[pl.BlockSpec((bm, k), lambda i: (i, 0)),
                  pl.BlockSpec((None, k, ql), lambda i: (l, 0, cq_col0 // ql), pipeline_mode=pl.Buffered(1)),
                  _resident((1, ql)), _resident((ql, n)), tab, tab, tab],
        out_specs=pl.BlockSpec((bm, n), lambda i: (i, 0)),
        out_shape=jax.ShapeDtypeStruct((m, n), BF16),
        compiler_params=_cparams(("arbitrary",),
                                 2 * (bm * k * 2 + bm * n * 2) + k * ql * 2 + ql * n * 2 + bm * (ql + n) * 8),
    )(h, w_in_bf, g_qn.reshape(1, ql), w_uq, *tabs)


def _mla_kv_kernel(h_ref, wc_ref, g_ref, wukv_ref, bf_ref, cos_ref, lo_ref, hi_ref, k_ref, v_ref, f_ref, *, kvl):
    ck = _dot(h_ref[...], wc_ref[...])
    kv = _dot(_rms(ck[:, :kvl], g_ref[...]).astype(BF16), wukv_ref[...])
    kr = _rotate(ck[:, kvl:kvl + LANES], cos_ref[...], lo_ref[...], hi_ref[...], QK_ROPE // 2).astype(BF16)
    for hd in range(B_HEADS):
        base = hd * B_QK_PAD
        k_ref[:, base:base + QK_NOPE] = kv[:, hd * QK_NOPE:(hd + 1) * QK_NOPE].astype(BF16)
        k_ref[:, base + QK_NOPE:base + B_QK_PAD] = kr
    v_ref[...] = kv[:, B_HEADS * QK_NOPE:].astype(BF16)
    f_ref[...] = ck[:, kvl + LANES:] + bf_ref[...]


def _mla_kv(h, w_c, g_kvn, w_ukv, b_f_row, tabs):
    m, k = h.shape
    kvl = g_kvn.shape[0]
    nc = w_c.shape[1]
    bm = _tile(m, (512, 256, 128))
    tab = pl.BlockSpec((bm, LANES), lambda i: (i, 0))
    nk = B_HEADS * B_QK_PAD
    return pl.pallas_call(
        functools.partial(_mla_kv_kernel, kvl=kvl),
        name="mla_kv",
        grid=(m // bm,),
        in_specs=[pl.BlockSpec((bm, k), lambda i: (i, 0)),
                  _resident((k, nc)), _resident((1, kvl)), _resident(w_ukv.shape), _resident((1, LANES)),
                  tab, tab, tab],
        out_specs=[pl.BlockSpec((bm, nk), lambda i: (i, 0)),
                   pl.BlockSpec((bm, B_WIDTH), lambda i: (i, 0)),
                   pl.BlockSpec((bm, LANES), lambda i: (i, 0))],
        out_shape=[jax.ShapeDtypeStruct((m, nk), BF16),
                   jax.ShapeDtypeStruct((m, B_WIDTH), BF16),
                   jax.ShapeDtypeStruct((m, LANES), F32)],
        compiler_params=_cparams(("arbitrary",),
                                 2 * (bm * k * 2 + k * nc * 2 + w_ukv.size * 2 + bm * (nk + B_WIDTH) * 2)
                                 + bm * (nc + nk + B_WIDTH) * 8),
    )(h, w_c, g_kvn.reshape(1, kvl), w_ukv, b_f_row, *tabs)


def _fox_cumsum_kernel(f_ref, o_ref):
    x = f_ref[...]
    y = (jnp.minimum(x, 0.0) - jnp.log(1.0 + jnp.exp(-jnp.abs(x)))) * LOG2E
    s = y.shape[1]
    lane = lax.broadcasted_iota(jnp.int32, y.shape, 1)
    shift = 1
    while shift < s:
        y = y + jnp.where(lane >= shift, pltpu.roll(y, shift, 1), 0.0)
        shift *= 2
    o_ref[...] = y


def _fox_cumsum(f_t):
    b, r, s = f_t.shape
    return pl.pallas_call(
        _fox_cumsum_kernel,
        name="fox_cumsum",
        grid=(b,),
        in_specs=[pl.BlockSpec((None, r, s), lambda i: (i, 0, 0))],
        out_specs=pl.BlockSpec((None, r, s), lambda i: (i, 0, 0)),
        out_shape=jax.ShapeDtypeStruct((b, r, s), F32),
        compiler_params=_cparams(("arbitrary",), 8 * r * s * 4),
    )(f_t)


def _flash_kernel(*refs, bq, bk, has_bias):
    if has_bias:
        q_ref, k_ref, v_ref, cq_ref, ck_ref, o_ref, acc_sc = refs
    else:
        q_ref, k_ref, v_ref, o_ref, acc_sc = refs
    n_diag = bq // bk
    n_full = pl.program_id(2) * n_diag

    def step(j, m, l, q0, masked):
        nq = bq - q0
        start = pl.multiple_of(j * bk, bk)
        s = _dot_nt(k_ref[pl.ds(start, bk), :], q_ref[q0:, :])
        if has_bias:
            ck = ck_ref[pl.ds(start, bk), :]
            s = s + cq_ref[:, q0:] - jnp.concatenate([ck] * (nq // LANES), axis=1)
        if masked:
            key = lax.broadcasted_iota(jnp.int32, (bk, nq), 0)
            qry = lax.broadcasted_iota(jnp.int32, (bk, nq), 1)
            s = jnp.where(key <= qry, s, NEG)
        m_prev = m[:, q0:]
        m_new = jnp.maximum(m_prev, jnp.max(s, axis=0, keepdims=True))
        alpha = jnp.exp2(m_prev - m_new)
        p = jnp.exp2(s - m_new)
        l_new = alpha * l[:, q0:] + jnp.sum(p, axis=0, keepdims=True)
        pv = lax.dot_general(v_ref[pl.ds(start, bk), :], p.astype(BF16), (((0,), (0,)), ((), ())),
                             preferred_element_type=F32)
        acc_sc[:, q0:] = alpha * acc_sc[:, q0:] + pv
        if q0:
            m_new = jnp.concatenate([m[:, :q0], m_new], axis=1)
            l_new = jnp.concatenate([l[:, :q0], l_new], axis=1)
        return m_new, l_new

    acc_sc[...] = jnp.zeros_like(acc_sc)
    init = (jnp.full((1, bq), NEG, F32), jnp.zeros((1, bq), F32))
    m, l = lax.fori_loop(0, n_full, lambda j, c: step(j, c[0], c[1], 0, False), init)
    for t in range(n_diag):
        m, l = step(n_full + t, m, l, t * bk, True)
    o_ref[...] = jnp.transpose(acc_sc[...] / l).astype(o_ref.dtype)


def _flash(q_arr, k_arr, v_arr, q_col0, k_col0, v_col0, dq, dv, heads, batch, seq, bias=None):
    bq = _tile(seq, (FLASH_BQ, 2048, 1024, 512, 256, 128))
    bk = _tile(bq, (FLASH_BK, 512, 256, 128))
    nq = seq // bq
    t = batch * seq
    in_specs = [pl.BlockSpec((bq, dq), lambda b, h, i: (b * nq + i, q_col0 + h)),
                pl.BlockSpec((seq, dq), lambda b, h, i: (b, k_col0 + h)),
                pl.BlockSpec((seq, dv), lambda b, h, i: (b, v_col0 + h))]
    args = [q_arr, k_arr, v_arr]
    if bias is not None:
        c_row, c_rep = bias
        in_specs += [pl.BlockSpec((None, 1, bq), lambda b, h, i: (b * F_ROWS + h, 0, i)),
                     pl.BlockSpec((None, seq, LANES), lambda b, h, i: (b * heads + h, 0, 0))]
        args += [c_row, c_rep]
    return pl.pallas_call(
        functools.partial(_flash_kernel, bq=bq, bk=bk, has_bias=bias is not None),
        name="flash_fox" if bias is not None else "flash_mla",
        grid=(batch, heads, nq),
        in_specs=in_specs,
        out_specs=pl.BlockSpec((bq, dv), lambda b, h, i: (b * nq + i, h)),
        out_shape=jax.ShapeDtypeStruct((t, heads * dv), BF16),
        scratch_shapes=[pltpu.VMEM((dv, bq), F32)],
        compiler_params=_cparams(("arbitrary", "arbitrary", "arbitrary"),
                                 2 * (seq * (dq + dv) * 2 + bq * (dq + dv) * 2 + seq * LANES * 4)
                                 + 4 * bq * bk * 4),
    )(*args)


def _dilated_kernel(q_ref, kp_ref, kc_ref, vp_ref, vc_ref, o_ref, lse_ref, kband, vband, *, rows, blk):
    n = pl.program_id(2)
    kband[0:blk, :] = kp_ref[...]
    kband[blk:, :] = kc_ref[...]
    vband[0:blk, :] = vp_ref[...]
    vband[blk:, :] = vc_ref[...]
    qi = lax.broadcasted_iota(jnp.int32, (blk, 2 * blk), 0)
    ki = lax.broadcasted_iota(jnp.int32, (blk, 2 * blk), 1)
    window = (ki >= qi) & (ki <= qi + blk)
    lane = lax.broadcasted_iota(jnp.int32, (blk, LANES), 1)

    def sub_block(a, carry):
        ro = pl.multiple_of(a * blk, blk)
        first_key = jnp.where(n * rows + ro > 0, 0, blk)
        mask = window & (ki >= first_key)
        lse_tile = jnp.zeros((blk, LANES), F32)
        for hd in range(A_HEADS):
            cs = slice(hd * HEAD_DIM, (hd + 1) * HEAD_DIM)
            s = _dot_nt(q_ref[pl.ds(ro, blk), cs], kband[pl.ds(ro, 2 * blk), cs])
            s = jnp.where(mask, s, NEG)
            m = jnp.max(s, axis=-1, keepdims=True)
            p = jnp.exp2(s - m)
            l = jnp.sum(p, axis=-1, keepdims=True)
            o = _dot(p.astype(BF16), vband[pl.ds(ro, 2 * blk), cs]) / l
            o_ref[pl.ds(ro, blk), cs] = o.astype(o_ref.dtype)
            lse_tile = jnp.where(lane == hd, m + jnp.log2(l), lse_tile)
        lse_ref[pl.ds(ro, blk), :] = lse_tile
        return carry

    lax.fori_loop(0, rows // blk, sub_block, 0)


def _dilated_group(qkv, row_width, batch, seq, window, dilation):
    blk = window // dilation
    sub = seq // dilation
    assert sub % blk == 0 and row_width % A_WIDTH == 0
    rows = _tile(sub, (4 * blk, 2 * blk, blk))
    nb = sub // rows
    rpb = rows // blk
    cpr = row_width // A_WIDTH
    t = batch * seq
    view = qkv.reshape(t // dilation, dilation * row_width)

    def cur(c):
        return pl.BlockSpec((rows, A_WIDTH), lambda b, r, n: (b * nb + n, cpr * r + c))

    def prev(c):
        return pl.BlockSpec((blk, A_WIDTH),
                            lambda b, r, n: (b * (sub // blk) + jnp.maximum(n * rpb - 1, 0), cpr * r + c))

    o, lse = pl.pallas_call(
        functools.partial(_dilated_kernel, rows=rows, blk=blk),
        name=f"dilated_d{dilation}",
        grid=(batch, dilation, nb),
        in_specs=[cur(0), prev(1), cur(1), prev(2), cur(2)],
        out_specs=[pl.BlockSpec((rows, A_WIDTH), lambda b, r, n: (b * nb + n, r)),
                   pl.BlockSpec((rows, LANES), lambda b, r, n: (b * nb + n, r))],
        out_shape=[jax.ShapeDtypeStruct((t // dilation, dilation * A_WIDTH), BF16),
                   jax.ShapeDtypeStruct((t // dilation, dilation * LANES), F32)],
        scratch_shapes=[pltpu.VMEM((rows + blk, A_WIDTH), BF16), pltpu.VMEM((rows + blk, A_WIDTH), BF16)],
        compiler_params=_cparams(("arbitrary", "arbitrary", "arbitrary"),
                                 2 * (3 * rows + 2 * blk) * A_WIDTH * 2 + 2 * rows * (A_WIDTH + LANES) * 4
                                 + 2 * (rows + blk) * A_WIDTH * 2 + 16 * blk * 2 * blk * 4),
    )(view, view, view, view, view)
    return o.reshape(t, A_WIDTH), lse.reshape(t, LANES)


DEINTERLEAVE_ROWS = 16 * max(d for _, d in A_PATTERNS)


def _deinterleave_kernel(x_ref, *refs, dilations):
    out_refs, scr = refs[:-1], refs[-1]
    rows = scr.shape[1]
    for g in range(scr.shape[0]):
        cols = slice(g * LANES, (g + 1) * LANES)
        scr[g] = x_ref[:, cols].astype(F32)
        for o_ref, d in zip(out_refs, dilations):
            for r in range(d):
                o_ref[r, :, cols] = scr[g, pl.ds(r, rows // d, stride=d), :].astype(o_ref.dtype)


def _deinterleave(x, batch, seq, dilations):
    t, w = x.shape
    rows = DEINTERLEAVE_ROWS
    nb = seq // rows
    outs = pl.pallas_call(
        functools.partial(_deinterleave_kernel, dilations=dilations),
        name="dilated_deinterleave",
        grid=(batch, nb),
        in_specs=[pl.BlockSpec((rows, w), lambda b, n: (b * nb + n, 0))],
        out_specs=[pl.BlockSpec((None, d, rows // d, w), lambda b, n: (b, 0, n, 0)) for d in dilations],
        out_shape=[jax.ShapeDtypeStruct((batch, d, seq // d, w), x.dtype) for d in dilations],
        scratch_shapes=[pltpu.VMEM((w // LANES, rows, LANES), F32)],
        compiler_params=_cparams(("arbitrary", "arbitrary"), rows * w * (4 + 4 + 4 * len(dilations) + 8)),
    )(x)
    return [o.reshape(t, w) for o in outs]


def _combine_kernel(*refs, dilations):
    n = len(dilations)
    o_refs, l_refs, out_ref, scratch = refs[:n], refs[n:2 * n], refs[2 * n], refs[2 * n + 1:]
    heads, lses = [], []
    for o_ref, l_ref, d in zip(o_refs, l_refs, dilations):
        if d == 1:
            heads.append(lambda hd, o_ref=o_ref: o_ref[:, hd * HEAD_DIM:(hd + 1) * HEAD_DIM])
            lses.append(l_ref[...])
            continue
        o_sc, l_sc = scratch[:2]
        scratch = scratch[2:]
        rows = o_sc.shape[1]
        for r in range(d):
            dst = pl.ds(r, rows // d, stride=d)
            l_sc[0, dst, :] = l_ref[r]
            for hd in range(A_HEADS):
                o_sc[hd, dst, :] = o_ref[r, :, hd * HEAD_DIM:(hd + 1) * HEAD_DIM].astype(F32)
        heads.append(lambda hd, o_sc=o_sc: o_sc[hd])
        lses.append(l_sc[0])
    mx = functools.reduce(jnp.maximum, lses)
    es = [jnp.exp2(a - mx) for a in lses]
    inv = 1.0 / functools.reduce(jnp.add, es)
    ws = [e * inv for e in es]
    for hd in range(A_HEADS):
        acc = functools.reduce(jnp.add, [w[:, hd:hd + 1] * head(hd) for w, head in zip(ws, heads)])
        out_ref[:, hd * HEAD_DIM:(hd + 1) * HEAD_DIM] = acc.astype(out_ref.dtype)


def _combine(outs, lses, batch, seq, dilations):
    t = batch * seq
    rows = DEINTERLEAVE_ROWS
    nb = seq // rows
    in_specs, args, scratch = [], [], []
    for width, arrs in ((A_WIDTH, outs), (LANES, lses)):
        for a, d in zip(arrs, dilations):
            if d == 1:
                in_specs.append(pl.BlockSpec((rows, width), lambda b, n: (b * nb + n, 0)))
                args.append(a)
            else:
                in_specs.append(pl.BlockSpec((None, d, rows // d, width), lambda b, n: (b, 0, n, 0)))
                args.append(a.reshape(batch, d, seq // d, width))
    for d in dilations:
        if d != 1:
            scratch += [pltpu.VMEM((A_HEADS, rows, LANES), F32), pltpu.VMEM((1, rows, LANES), F32)]
    return pl.pallas_call(
        functools.partial(_combine_kernel, dilations=dilations),
        name="dilated_combine",
        grid=(batch, nb),
        in_specs=in_specs,
        out_specs=pl.BlockSpec((rows, A_WIDTH), lambda b, n: (b * nb + n, 0)),
        out_shape=jax.ShapeDtypeStruct((t, A_WIDTH), BF16),
        scratch_shapes=scratch,
        compiler_params=_cparams(("arbitrary", "arbitrary"),
                                 rows * (A_WIDTH + LANES) * 4 * (3 * len(dilations) + 4)),
    )(*args)


def _merge_kernel(h_ref, oa_ref, ob_ref, oc_ref, wg0, wg1, wg2, wa, wb, wc, o_ref):
    h = h_ref[...]
    acc = jax.nn.sigmoid(_dot(h, wg0[...])) * _dot(oa_ref[...], wa[...])
    acc += jax.nn.sigmoid(_dot(h, wg1[...])) * _dot(ob_ref[...], wb[...])
    acc += jax.nn.sigmoid(_dot(h, wg2[...])) * _dot(oc_ref[...], wc[...])
    o_ref[...] = acc.astype(o_ref.dtype)


def _merge(h, oa, ob, oc, w_gate, w_a, w_b, w_c):
    m, d = h.shape
    bm = _tile(m, (512, 256, 128))
    bn = _tile(d, (512, 256, 128))
    nj = d // bn

    def rows(width):
        return pl.BlockSpec((bm, width), lambda i, j: (i, 0))

    def gate(g):
        return pl.BlockSpec((d, bn), lambda i, j: (0, g * nj + j))

    def branch(width):
        return pl.BlockSpec((width, bn), lambda i, j: (0, j))

    k_all = 3 * d + A_WIDTH + B_WIDTH + C_WIDTH
    return pl.pallas_call(
        _merge_kernel,
        name="gate_merge",
        grid=(m // bm, nj),
        in_specs=[rows(d), rows(A_WIDTH), rows(B_WIDTH), rows(C_WIDTH), gate(0), gate(1), gate(2),
                  branch(A_WIDTH), branch(B_WIDTH), branch(C_WIDTH)],
        out_specs=pl.BlockSpec((bm, bn), lambda i, j: (i, j)),
        out_shape=jax.ShapeDtypeStruct((m, d), BF16),
        compiler_params=_cparams(("arbitrary", "arbitrary"),
                                 2 * (bm * (d + A_WIDTH + B_WIDTH + C_WIDTH) * 2 + k_all * bn * 2 + bm * bn * 2)
                                 + 8 * bm * bn * 4),
    )(h, oa, ob, oc, w_gate, w_gate, w_gate, w_a, w_b, w_c)


def _ffn_in_kernel(h_ref, wa_ref, wb_ref, o_ref):
    h = h_ref[...]
    a = _dot(h, wa_ref[...])
    b = _dot(h, wb_ref[...])
    o_ref[...] = (a * jax.nn.sigmoid(a) * b).astype(o_ref.dtype)


def _ffn_in(h, w_in, d_ff):
    m, d = h.shape
    bm = _tile(m, (1024, 512, 256, 128))
    bn = _tile(d_ff, (512, 256, 128))
    nj = d_ff // bn
    return pl.pallas_call(
        _ffn_in_kernel,
        name="ffn_in",
        grid=(m // bm, nj),
        in_specs=[pl.BlockSpec((bm, d), lambda i, j: (i, 0)),
                  pl.BlockSpec((d, bn), lambda i, j: (0, j)),
                  pl.BlockSpec((d, bn), lambda i, j: (0, nj + j))],
        out_specs=pl.BlockSpec((bm, bn), lambda i, j: (i, j)),
        out_shape=jax.ShapeDtypeStruct((m, d_ff), BF16),
        compiler_params=_cparams(("arbitrary", "arbitrary"),
                                 2 * (bm * d * 2 + 2 * d * bn * 2 + bm * bn * 2) + 4 * bm * bn * 4),
    )(h, w_in, w_in)


def _layer(x2, h, ada_l, ada_next, batch, seq, tabs_a, tabs_b, w_in_bf, l, w, alpha, last):
    d = x2.shape[1]
    t = batch * seq
    sh1, sc1, gt1, sh2, sc2, gt2 = [a.reshape(batch, 1, d) for a in jnp.split(ada_l, 6, axis=-1)]
    del sh1, sc1

    qkv_a = _proj_ac(h, w_in_bf, l, SCALE_A, tabs_a)
    qkv_c = _matmul(h, w["c"], BF16, bn_prefs=(768, 512, 256, 128), scale=SCALE_A, scaled_cols=C_WIDTH)
    q_b = _mla_q(h, w_in_bf, l, w["cq_col0"], w["g_qn"], w["uq"], tabs_b)
    k_b, v_b, f_logit = _mla_kv(h, w["ckv"], w["g_kvn"], w["ukv"], w["b_f"], tabs_b)

    dilations = tuple(dil for _, dil in A_PATTERNS)
    strided = [dil for dil in dilations if dil != 1]
    copies = dict(zip(strided, _deinterleave(qkv_a, batch, seq, strided)))
    outs, lses = [], []
    for window, dil in A_PATTERNS:
        src = qkv_a if dil == 1 else copies[dil]
        o, l = _dilated_group(src, src.shape[1], batch * dil, seq // dil, window // dil, 1)
        outs.append(o)
        lses.append(l)
    o_a = _combine(outs, lses, batch, seq, dilations)

    o_b = _flash(q_b, k_b, v_b, 0, 0, 0, B_QK_PAD, V_DIM, B_HEADS, batch, seq)

    f_t = f_logit[:, :F_ROWS].reshape(batch, seq, F_ROWS).transpose(0, 2, 1)
    c_t = _fox_cumsum(f_t)
    c_row = c_t.reshape(batch * F_ROWS, 1, seq)
    c_rep = jnp.broadcast_to(c_t[:, :C_HEADS, :, None], (batch, C_HEADS, seq, LANES))
    c_rep = c_rep.reshape(batch * C_HEADS, seq, LANES)
    nh = A_WIDTH // HEAD_DIM
    o_c = _flash(qkv_c, qkv_c, qkv_c, 0, nh, 2 * nh, HEAD_DIM, HEAD_DIM, C_HEADS, batch, seq,
                 bias=(c_row, c_rep))

    merged = _merge(h, o_a, o_b, o_c, w["gate"], w["br_a"], w["br_b"], w["br_c"])
    y = _matmul(merged, w["o"], BF16)
    x2, h2 = _res_ln(x2, y, gt1, w["ln1_g"], w["ln1_b"], sc2, sh2, batch, alpha, True)

    act = _ffn_in(h2, w["ffn_in"], w["d_ff"])
    nk = 2 if (w["d_ff"] // 2) % LANES == 0 else 1
    y = _matmul_ksplit(act, w["ffn_out"], BF16, nk)
    if last:
        x2, hn = _res_ln(x2, y, gt2, w["ln2_g"], w["ln2_b"], sc2, sh2, batch, alpha, False)
    else:
        sh1n, sc1n = [a.reshape(batch, 1, d) for a in jnp.split(ada_next, 6, axis=-1)[:2]]
        x2, hn = _res_ln(x2, y, gt2, w["ln2_g"], w["ln2_b"], sc1n, sh1n, batch, alpha, True)
    return x2, hn


SCALE_A = HEAD_DIM ** -0.5 * LOG2E
SCALE_B = (QK_NOPE + QK_ROPE) ** -0.5 * LOG2E


def _prep_weights(l, w_in_bf, b_f, g_qn, w_uq, g_kvn, w_ukv, w_br_a, w_br_b, w_br_c, w_o,
                  ln1_g, ln1_b, w_ffn_in, w_ffn_out, ln2_g, ln2_b):
    d = w_in_bf.shape[1]
    ql = g_qn.shape[1]
    kvl = g_kvn.shape[1]
    wi = w_in_bf[l]
    widths = (A_WIDTH, A_WIDTH, A_WIDTH, ql, kvl, QK_ROPE, C_WIDTH, C_WIDTH, C_WIDTH, C_HEADS, N_BRANCH * d)
    offs = [int(o) for o in np.concatenate([[0], np.cumsum(widths)])]
    ckv, kr, fl = [wi[:, offs[i]:offs[i + 1]] for i in (4, 5, 9)]
    zpad = lambda n: jnp.zeros((d, n), BF16)
    w_ckv = jnp.concatenate([ckv, kr, zpad(LANES - QK_ROPE), fl, zpad(LANES - C_HEADS)], axis=1)
    uq = w_uq[l].reshape(ql, B_HEADS, QK_NOPE + QK_ROPE) * SCALE_B
    uq = jnp.pad(uq, ((0, 0), (0, 0), (0, B_QK_PAD - QK_NOPE - QK_ROPE))).reshape(ql, B_HEADS * B_QK_PAD)
    ukv = w_ukv[l].reshape(kvl, B_HEADS, 2, QK_NOPE).transpose(0, 2, 1, 3).reshape(kvl, 2 * B_HEADS * QK_NOPE)
    b_f_row = jnp.pad(b_f[l], (0, LANES - C_HEADS)).reshape(1, LANES)
    return dict(cq_col0=offs[3], c=wi[:, offs[6]:offs[9]], g_qn=g_qn[l], uq=uq.astype(BF16), ckv=w_ckv,
                g_kvn=g_kvn[l], ukv=ukv.astype(BF16), b_f=b_f_row, gate=wi[:, offs[10]:],
                br_a=w_br_a[l].astype(BF16), br_b=w_br_b[l].astype(BF16), br_c=w_br_c[l].astype(BF16),
                o=w_o[l].astype(BF16), ln1_g=ln1_g[l], ln1_b=ln1_b[l], ffn_in=w_ffn_in[l].astype(BF16),
                ffn_out=w_ffn_out[l].astype(BF16), d_ff=w_ffn_out.shape[1], ln2_g=ln2_g[l], ln2_b=ln2_b[l])


def kernel(x, c, positions, w_ada, b_ada, w_in, b_f, g_qn, w_uq, g_kvn, w_ukv, w_br_a, w_br_b, w_br_c, w_o,
           ln1_g, ln1_b, w_ffn_in, w_ffn_out, ln2_g, ln2_b):
    batch, seq, d = x.shape
    depth = w_ada.shape[0]
    alpha = (2.0 * depth) ** 0.25
    t = batch * seq
    x2 = x.reshape(t, d)
    ada = _ada(c, w_ada, b_ada)
    pos_col = positions.astype(F32).reshape(t, 1)
    tabs_a = _rope_tables(pos_col, PARTIAL_ROPE_DIM)
    tabs_b = _rope_tables(pos_col, QK_ROPE)
    sh1, sc1 = [a.reshape(batch, 1, d) for a in jnp.split(ada[0], 6, axis=-1)[:2]]
    h = _modulate(x2, sc1, sh1, batch)
    w_in_bf = w_in.astype(BF16)
    for l in range(depth):
        w = _prep_weights(l, w_in_bf, b_f, g_qn, w_uq, g_kvn, w_ukv, w_br_a, w_br_b, w_br_c, w_o,
                          ln1_g, ln1_b, w_ffn_in, w_ffn_out, ln2_g, ln2_b)
        last = l == depth - 1
        x2, h = _layer(x2, h, ada[l], None if last else ada[l + 1], batch, seq, tabs_a, tabs_b, w_in_bf, l, w,
                       alpha, last)
    return x2.reshape(batch, seq, d)
```

```python
import functools
import math

import jax
import jax.numpy as jnp
import numpy as np
from jax import lax
from jax.experimental import pallas as pl
from jax.experimental.pallas import tpu as pltpu

HEAD_DIM = 128
ROPE_THETA = 500000.0
PARTIAL_ROPE_DIM = HEAD_DIM // 4
A_HEADS = 12
A_PATTERNS = ((128, 1), (512, 4), (2048, 16))
B_HEADS = 8
QK_NOPE = 128
QK_ROPE = 64
V_DIM = 128
C_HEADS = 12
N_BRANCH = 3
A_WIDTH = A_HEADS * HEAD_DIM
B_WIDTH = B_HEADS * V_DIM
C_WIDTH = C_HEADS * HEAD_DIM
B_QK_PAD = 256
F_ROWS = 16
FLASH_BQ = 4096
FLASH_BK = 1024
NEG = -1e30
LOG2E = math.log2(math.e)
LANES = 128
V7X_VMEM_CAP = 60 * 1024 * 1024

BF16 = jnp.bfloat16
F32 = jnp.float32


def _cparams(semantics, vmem_estimate):
    limit = int(min(max(vmem_estimate * 5 // 4, 32 * 1024 * 1024), V7X_VMEM_CAP))
    return pltpu.CompilerParams(dimension_semantics=semantics, vmem_limit_bytes=limit)


def _tile(n, prefs):
    for p in prefs:
        if n % p == 0:
            return p
    return n


def _resident(shape):
    return pl.BlockSpec(shape, lambda i: (0,) * len(shape), pipeline_mode=pl.Buffered(1))


def _dot(a, b):
    return jnp.dot(a, b, preferred_element_type=F32)


def _dot_nt(a, b):
    return lax.dot_general(a, b, (((1,), (1,)), ((), ())), preferred_element_type=F32)


def _rotate(t, cos, sin_lo, sin_hi, half):
    return t * cos + pltpu.roll(t, half, 1) * sin_hi + pltpu.roll(t, LANES - half, 1) * sin_lo


def _ada_kernel(c_ref, w_ref, b_ref, o_ref, acc_sc):
    kblk = pl.program_id(1)
    nb, kb = c_ref.shape[0], c_ref.shape[1]

    @pl.when(kblk == 0)
    def _():
        acc_sc[...] = jnp.zeros_like(acc_sc)

    cv = c_ref[...]
    s = cv * jax.nn.sigmoid(cv)
    for g in range(w_ref.shape[1] // LANES):
        cols = slice(g * LANES, (g + 1) * LANES)
        wg = w_ref[:, cols]
        for b in range(nb):
            acc_sc[b, :, cols] += jnp.sum((wg * s[b]).reshape(kb // 8, 8, LANES), axis=0)

    @pl.when(kblk == pl.num_programs(1) - 1)
    def _():
        for b in range(nb):
            o_ref[b:b + 1, :] = jnp.sum(acc_sc[b], axis=0, keepdims=True) + b_ref[...]


def _ada(c, w_ada, b_ada):
    depth, d, n = w_ada.shape
    b = c.shape[0]
    kb = _tile(d, (128, 64, 32, 16, 8))
    c_rep = jnp.broadcast_to(c[:, :, None], (b, d, LANES))
    return pl.pallas_call(
        _ada_kernel,
        name="ada",
        grid=(depth, d // kb),
        in_specs=[pl.BlockSpec((b, kb, LANES), lambda l, k: (0, k, 0)),
                  pl.BlockSpec((None, kb, n), lambda l, k: (l, k, 0)),
                  pl.BlockSpec((None, 1, n), lambda l, k: (l, 0, 0))],
        out_specs=pl.BlockSpec((None, b, n), lambda l, k: (l, 0, 0)),
        out_shape=jax.ShapeDtypeStruct((depth, b, n), F32),
        scratch_shapes=[pltpu.VMEM((b, 8, n), F32)],
        compiler_params=_cparams(("arbitrary", "arbitrary"), 2 * kb * n * 4 + 4 * b * 8 * n * 4),
    )(c_rep, w_ada, b_ada.reshape(depth, 1, n))


def _rope_table_kernel(pos_ref, freq_ref, mc_ref, m1_ref, mlo_ref, mhi_ref, cos_ref, lo_ref, hi_ref):
    ang = pos_ref[...] * freq_ref[...]
    cs = jnp.cos(ang)
    sn = jnp.sin(ang)
    cos_ref[...] = cs * mc_ref[...] + m1_ref[...]
    lo_ref[...] = sn * mlo_ref[...]
    hi_ref[...] = sn * mhi_ref[...]


def _rope_tables(pos_col, rot_dim):
    t = pos_col.shape[0]
    half = rot_dim // 2
    inv = np.exp(-math.log(ROPE_THETA) * np.arange(half, dtype=np.float32) * np.float32(2.0 / rot_dim))
    lane = np.arange(LANES)
    freq = np.where(lane < rot_dim, inv[lane % half], 0.0).astype(np.float32)[None]
    m_cos = (lane < rot_dim).astype(np.float32)[None]
    m_one = (lane >= rot_dim).astype(np.float32)[None]
    m_lo = np.where(lane < half, -1.0, 0.0).astype(np.float32)[None]
    m_hi = np.where((lane >= half) & (lane < rot_dim), 1.0, 0.0).astype(np.float32)[None]
    bm = _tile(t, (2048, 1024, 512, 256, 128))
    row = pl.BlockSpec((1, LANES), lambda i: (0, 0))
    tab = pl.BlockSpec((bm, LANES), lambda i: (i, 0))
    shp = jax.ShapeDtypeStruct((t, LANES), F32)
    return pl.pallas_call(
        _rope_table_kernel,
        name="rope_tables",
        grid=(t // bm,),
        in_specs=[pl.BlockSpec((bm, 1), lambda i: (i, 0)), row, row, row, row, row],
        out_specs=[tab, tab, tab],
        out_shape=[shp, shp, shp],
        compiler_params=_cparams(("arbitrary",), 16 * bm * LANES * 4),
    )(pos_col, jnp.asarray(freq), jnp.asarray(m_cos), jnp.asarray(m_one), jnp.asarray(m_lo), jnp.asarray(m_hi))


def _mod_kernel(x_ref, sc_ref, sh_ref, o_ref):
    o_ref[...] = (x_ref[...] * (1.0 + sc_ref[...]) + sh_ref[...]).astype(o_ref.dtype)


def _modulate(x2, sc, sh, batch):
    t, d = x2.shape
    s = t // batch
    bm = _tile(s, (512, 256, 128))
    nb = s // bm
    vec = pl.BlockSpec((None, 1, d), lambda b, i: (b, 0, 0))
    return pl.pallas_call(
        _mod_kernel,
        name="modulate",
        grid=(batch, nb),
        in_specs=[pl.BlockSpec((bm, d), lambda b, i: (b * nb + i, 0)), vec, vec],
        out_specs=pl.BlockSpec((bm, d), lambda b, i: (b * nb + i, 0)),
        out_shape=jax.ShapeDtypeStruct((t, d), BF16),
        compiler_params=_cparams(("arbitrary", "arbitrary"), 2 * bm * d * 6),
    )(x2, sc, sh)


def _res_ln_kernel(x_ref, y_ref, gt_ref, g_ref, b_ref, sc_ref, sh_ref, xo_ref, *ho_ref, alpha):
    z = alpha * x_ref[...] + (1.0 + gt_ref[...]) * y_ref[...].astype(F32)
    mu = jnp.mean(z, axis=-1, keepdims=True)
    zc = z - mu
    var = jnp.mean(zc * zc, axis=-1, keepdims=True)
    xn = zc * lax.rsqrt(var + 1e-5) * g_ref[...] + b_ref[...]
    xo_ref[...] = xn
    if ho_ref:
        ho_ref[0][...] = (xn * (1.0 + sc_ref[...]) + sh_ref[...]).astype(BF16)


def _res_ln(x2, y2, gt, ln_g, ln_b, sc, sh, batch, alpha, with_h):
    t, d = x2.shape
    s = t // batch
    bm = _tile(s, (256, 128))
    nb = s // bm
    vec = pl.BlockSpec((None, 1, d), lambda b, i: (b, 0, 0))
    par = pl.BlockSpec((1, d), lambda b, i: (0, 0))
    blk = pl.BlockSpec((bm, d), lambda b, i: (b * nb + i, 0))
    out_specs = [blk, blk] if with_h else [blk]
    out_shape = [jax.ShapeDtypeStruct((t, d), F32)]
    if with_h:
        out_shape.append(jax.ShapeDtypeStruct((t, d), BF16))
    outs = pl.pallas_call(
        functools.partial(_res_ln_kernel, alpha=alpha),
        name="res_ln",
        grid=(batch, nb),
        in_specs=[blk, blk, vec, par, par, vec, vec],
        out_specs=out_specs,
        out_shape=out_shape,
        compiler_params=_cparams(("arbitrary", "arbitrary"), 2 * bm * d * 14 + 6 * bm * d * 4),
    )(x2, y2, gt, ln_g.reshape(1, d), ln_b.reshape(1, d), sc, sh)
    return (outs[0], outs[1]) if with_h else (outs[0], None)


def _mm_kernel(x_ref, w_ref, o_ref, *, scale, n_scaled):
    acc = _dot(x_ref[...], w_ref[...])
    if n_scaled:
        acc = acc * jnp.where(pl.program_id(1) < n_scaled, scale, 1.0)
    o_ref[...] = acc.astype(o_ref.dtype)


def _matmul(x, w, out_dtype, bm_prefs=(1024, 512, 256, 128), bn_prefs=(1024, 512, 256, 128), scale=1.0,
            scaled_cols=0):
    m, k = x.shape
    n = w.shape[1]
    bm = _tile(m, bm_prefs)
    bn = _tile(n, bn_prefs)
    assert scaled_cols % bn == 0
    osz = jnp.dtype(out_dtype).itemsize
    return pl.pallas_call(
        functools.partial(_mm_kernel, scale=scale, n_scaled=scaled_cols // bn),
        name="matmul",
        grid=(m // bm, n // bn),
        in_specs=[pl.BlockSpec((bm, k), lambda i, j: (i, 0)),
                  pl.BlockSpec((k, bn), lambda i, j: (0, j))],
        out_specs=pl.BlockSpec((bm, bn), lambda i, j: (i, j)),
        out_shape=jax.ShapeDtypeStruct((m, n), out_dtype),
        compiler_params=_cparams(("arbitrary", "arbitrary"),
                                 2 * (bm * k * 2 + k * bn * 2 + bm * bn * osz) + bm * bn * 4),
    )(x, w)


def _proj_ac_kernel(x_ref, w_ref, cos_ref, lo_ref, hi_ref, o_ref, *, n_rot_tiles, half, scale):
    j = pl.program_id(1)
    acc = _dot(x_ref[...], w_ref[...]) * jnp.where(j < n_rot_tiles // 2, scale, 1.0)

    @pl.when(j < n_rot_tiles)
    def _():
        cs, lo, hi = cos_ref[...], lo_ref[...], hi_ref[...]
        for g in range(acc.shape[1] // LANES):
            sl = slice(g * LANES, (g + 1) * LANES)
            o_ref[:, sl] = _rotate(acc[:, sl], cs, lo, hi, half).astype(o_ref.dtype)

    @pl.when(j >= n_rot_tiles)
    def _():
        o_ref[...] = acc.astype(o_ref.dtype)


def _proj_ac(h, w_in_bf, l, scale, tabs):
    m, k = h.shape
    n = 3 * A_WIDTH
    bm = _tile(m, (1024, 512, 256, 128))
    bn = _tile(A_WIDTH, (768, 512, 256, 128))
    tab = pl.BlockSpec((bm, LANES), lambda i, j: (i, 0))
    return pl.pallas_call(
        functools.partial(_proj_ac_kernel, n_rot_tiles=2 * A_WIDTH // bn, half=PARTIAL_ROPE_DIM // 2, scale=scale),
        name="proj_ac",
        grid=(m // bm, n // bn),
        in_specs=[pl.BlockSpec((bm, k), lambda i, j: (i, 0)),
                  pl.BlockSpec((k, bn), lambda i, j: (l, j)), tab, tab, tab],
        out_specs=pl.BlockSpec((bm, bn), lambda i, j: (i, j)),
        out_shape=jax.ShapeDtypeStruct((m, n), BF16),
        compiler_params=_cparams(("arbitrary", "arbitrary"),
                                 2 * (bm * k * 2 + k * bn * 2 + bm * bn * 2 + 3 * bm * LANES * 4) + 2 * bm * bn * 4),
    )(h, w_in_bf, *tabs)


def _rms(x, g, eps=1e-6):
    return x * lax.rsqrt(jnp.mean(x * x, axis=-1, keepdims=True) + eps) * g


def _mla_q_kernel(h_ref, wcq_ref, g_ref, wuq_ref, cos_ref, lo_ref, hi_ref, o_ref):
    cq = _dot(h_ref[...], wcq_ref[...])
    q = _dot(_rms(cq, g_ref[...]).astype(BF16), wuq_ref[...])
    cs, lo, hi = cos_ref[...], lo_ref[...], hi_ref[...]
    for hd in range(B_HEADS):
        base = hd * B_QK_PAD
        o_ref[:, base:base + QK_NOPE] = q[:, base:base + QK_NOPE].astype(BF16)
        rope = _rotate(q[:, base + QK_NOPE:base + B_QK_PAD], cs, lo, hi, QK_ROPE // 2)
        o_ref[:, base + QK_NOPE:base + B_QK_PAD] = rope.astype(BF16)


def _mla_q(h, w_in_bf, l, cq_col0, g_qn, w_uq, tabs):
    m, k = h.shape
    ql = g_qn.shape[0]
    n = w_uq.shape[1]
    assert cq_col0 % ql == 0
    bm = _tile(m, (512, 256, 128))
    tab = pl.BlockSpec((bm, LANES), lambda i: (i, 0))
    return pl.pallas_call(
        _mla_q_kernel,
        name="mla_q",
        grid=(m // bm,),
        in_specs=[pl.BlockSpec((bm, k), lambda i: (i, 0)),
                  pl.BlockSpec((k, ql), lambda i: (l, cq_col0 // ql), pipeline_mode=pl.Buffered(1)),
                  _resident((1, ql)), _resident((ql, n)), tab, tab, tab],
        out_specs=pl.BlockSpec((bm, n), lambda i: (i, 0)),
        out_shape=jax.ShapeDtypeStruct((m, n), BF16),
        compiler_params=_cparams(("arbitrary",),
                                 2 * (bm * k * 2 + bm * n * 2) + k * ql * 2 + ql * n * 2 + bm * (ql + n) * 8),
    )(h, w_in_bf, g_qn.reshape(1, ql), w_uq, *tabs)


def _mla_kv_kernel(h_ref, wc_ref, g_ref, wukv_ref, bf_ref, cos_ref, lo_ref, hi_ref, k_ref, v_ref, f_ref, *, kvl):
    ck = _dot(h_ref[...], wc_ref[...])
    kv = _dot(_rms(ck[:, :kvl], g_ref[...]).astype(BF16), wukv_ref[...])
    kr = _rotate(ck[:, kvl:kvl + LANES], cos_ref[...], lo_ref[...], hi_ref[...], QK_ROPE // 2).astype(BF16)
    for hd in range(B_HEADS):
        base = hd * B_QK_PAD
        k_ref[:, base:base + QK_NOPE] = kv[:, hd * QK_NOPE:(hd + 1) * QK_NOPE].astype(BF16)
        k_ref[:, base + QK_NOPE:base + B_QK_PAD] = kr
    v_ref[...] = kv[:, B_HEADS * QK_NOPE:].astype(BF16)
    f_ref[...] = ck[:, kvl + LANES:] + bf_ref[...]


def _mla_kv(h, w_c, g_kvn, w_ukv, b_f_row, tabs):
    m, k = h.shape
    kvl = g_kvn.shape[0]
    nc = w_c.shape[1]
    bm = _tile(m, (512, 256, 128))
    tab = pl.BlockSpec((bm, LANES), lambda i: (i, 0))
    nk = B_HEADS * B_QK_PAD
    return pl.pallas_call(
        functools.partial(_mla_kv_kernel, kvl=kvl),
        name="mla_kv",
        grid=(m // bm,),
        in_specs=[pl.BlockSpec((bm, k), lambda i: (i, 0)),
                  _resident((k, nc)), _resident((1, kvl)), _resident(w_ukv.shape), _resident((1, LANES)),
                  tab, tab, tab],
        out_specs=[pl.BlockSpec((bm, nk), lambda i: (i, 0)),
                   pl.BlockSpec((bm, B_WIDTH), lambda i: (i, 0)),
                   pl.BlockSpec((bm, LANES), lambda i: (i, 0))],
        out_shape=[jax.ShapeDtypeStruct((m, nk), BF16),
                   jax.ShapeDtypeStruct((m, B_WIDTH), BF16),
                   jax.ShapeDtypeStruct((m, LANES), F32)],
        compiler_params=_cparams(("arbitrary",),
                                 2 * (bm * k * 2 + k * nc * 2 + w_ukv.size * 2 + bm * (nk + B_WIDTH) * 2)
                                 + bm * (nc + nk + B_WIDTH) * 8),
    )(h, w_c, g_kvn.reshape(1, kvl), w_ukv, b_f_row, *tabs)


def _fox_cumsum_kernel(f_ref, o_ref):
    x = f_ref[...]
    y = (jnp.minimum(x, 0.0) - jnp.log(1.0 + jnp.exp(-jnp.abs(x)))) * LOG2E
    s = y.shape[1]
    lane = lax.broadcasted_iota(jnp.int32, y.shape, 1)
    shift = 1
    while shift < s:
        y = y + jnp.where(lane >= shift, pltpu.roll(y, shift, 1), 0.0)
        shift *= 2
    o_ref[...] = y


def _fox_cumsum(f_t):
    b, r, s = f_t.shape
    return pl.pallas_call(
        _fox_cumsum_kernel,
        name="fox_cumsum",
        grid=(b,),
        in_specs=[pl.BlockSpec((None, r, s), lambda i: (i, 0, 0))],
        out_specs=pl.BlockSpec((None, r, s), lambda i: (i, 0, 0)),
        out_shape=jax.ShapeDtypeStruct((b, r, s), F32),
        compiler_params=_cparams(("arbitrary",), 8 * r * s * 4),
    )(f_t)


def _flash_kernel(*refs, bq, bk, has_bias):
    if has_bias:
        q_ref, k_ref, v_ref, cq_ref, ck_ref, o_ref, acc_sc = refs
    else:
        q_ref, k_ref, v_ref, o_ref, acc_sc = refs
    n_diag = bq // bk
    n_full = pl.program_id(2) * n_diag

    def step(j, m, l, q0, masked):
        nq = bq - q0
        start = pl.multiple_of(j * bk, bk)
        s = _dot_nt(k_ref[pl.ds(start, bk), :], q_ref[q0:, :])
        if has_bias:
            ck = ck_ref[pl.ds(start, bk), :]
            s = s + cq_ref[:, q0:] - jnp.concatenate([ck] * (nq // LANES), axis=1)
        if masked:
            key = lax.broadcasted_iota(jnp.int32, (bk, nq), 0)
            qry = lax.broadcasted_iota(jnp.int32, (bk, nq), 1)
            s = jnp.where(key <= qry, s, NEG)
        m_prev = m[:, q0:]
        m_new = jnp.maximum(m_prev, jnp.max(s, axis=0, keepdims=True))
        alpha = jnp.exp2(m_prev - m_new)
        p = jnp.exp2(s - m_new)
        l_new = alpha * l[:, q0:] + jnp.sum(p, axis=0, keepdims=True)
        pv = lax.dot_general(v_ref[pl.ds(start, bk), :], p.astype(BF16), (((0,), (0,)), ((), ())),
                             preferred_element_type=F32)
        acc_sc[:, q0:] = alpha * acc_sc[:, q0:] + pv
        if q0:
            m_new = jnp.concatenate([m[:, :q0], m_new], axis=1)
            l_new = jnp.concatenate([l[:, :q0], l_new], axis=1)
        return m_new, l_new

    acc_sc[...] = jnp.zeros_like(acc_sc)
    init = (jnp.full((1, bq), NEG, F32), jnp.zeros((1, bq), F32))
    m, l = lax.fori_loop(0, n_full, lambda j, c: step(j, c[0], c[1], 0, False), init)
    for t in range(n_diag):
        m, l = step(n_full + t, m, l, t * bk, True)
    o_ref[...] = jnp.transpose(acc_sc[...] / l).astype(o_ref.dtype)


def _flash(q_arr, k_arr, v_arr, q_col0, k_col0, v_col0, dq, dv, heads, batch, seq, bias=None):
    bq = _tile(seq, (FLASH_BQ, 2048, 1024, 512, 256, 128))
    bk = _tile(bq, (FLASH_BK, 512, 256, 128))
    nq = seq // bq
    t = batch * seq
    in_specs = [pl.BlockSpec((bq, dq), lambda b, h, i: (b * nq + i, q_col0 + h)),
                pl.BlockSpec((seq, dq), lambda b, h, i: (b, k_col0 + h)),
                pl.BlockSpec((seq, dv), lambda b, h, i: (b, v_col0 + h))]
    args = [q_arr, k_arr, v_arr]
    if bias is not None:
        c_row, c_rep = bias
        in_specs += [pl.BlockSpec((None, 1, bq), lambda b, h, i: (b * F_ROWS + h, 0, i)),
                     pl.BlockSpec((None, seq, LANES), lambda b, h, i: (b * heads + h, 0, 0))]
        args += [c_row, c_rep]
    return pl.pallas_call(
        functools.partial(_flash_kernel, bq=bq, bk=bk, has_bias=bias is not None),
        name="flash_fox" if bias is not None else "flash_mla",
        grid=(batch, heads, nq),
        in_specs=in_specs,
        out_specs=pl.BlockSpec((bq, dv), lambda b, h, i: (b * nq + i, h)),
        out_shape=jax.ShapeDtypeStruct((t, heads * dv), BF16),
        scratch_shapes=[pltpu.VMEM((dv, bq), F32)],
        compiler_params=_cparams(("arbitrary", "arbitrary", "arbitrary"),
                                 2 * (seq * (dq + dv) * 2 + bq * (dq + dv) * 2 + seq * LANES * 4)
                                 + 4 * bq * bk * 4),
    )(*args)


def _dilated_kernel(q_ref, kp_ref, kc_ref, vp_ref, vc_ref, o_ref, lse_ref, kband, vband, *, rows, blk):
    n = pl.program_id(2)
    kband[0:blk, :] = kp_ref[...]
    kband[blk:, :] = kc_ref[...]
    vband[0:blk, :] = vp_ref[...]
    vband[blk:, :] = vc_ref[...]
    qi = lax.broadcasted_iota(jnp.int32, (blk, 2 * blk), 0)
    ki = lax.broadcasted_iota(jnp.int32, (blk, 2 * blk), 1)
    window = (ki >= qi) & (ki <= qi + blk)
    lane = lax.broadcasted_iota(jnp.int32, (blk, LANES), 1)

    def sub_block(a, carry):
        ro = pl.multiple_of(a * blk, blk)
        first_key = jnp.where(n * rows + ro > 0, 0, blk)
        mask = window & (ki >= first_key)
        lse_tile = jnp.zeros((blk, LANES), F32)
        for hd in range(A_HEADS):
            cs = slice(hd * HEAD_DIM, (hd + 1) * HEAD_DIM)
            s = _dot_nt(q_ref[pl.ds(ro, blk), cs], kband[pl.ds(ro, 2 * blk), cs])
            s = jnp.where(mask, s, NEG)
            m = jnp.max(s, axis=-1, keepdims=True)
            p = jnp.exp2(s - m)
            l = jnp.sum(p, axis=-1, keepdims=True)
            o = _dot(p.astype(BF16), vband[pl.ds(ro, 2 * blk), cs]) / l
            o_ref[pl.ds(ro, blk), cs] = o.astype(o_ref.dtype)
            lse_tile = jnp.where(lane == hd, m + jnp.log2(l), lse_tile)
        lse_ref[pl.ds(ro, blk), :] = lse_tile
        return carry

    lax.fori_loop(0, rows // blk, sub_block, 0)


def _dilated_group(qkv, row_width, batch, seq, window, dilation):
    blk = window // dilation
    sub = seq // dilation
    assert sub % blk == 0 and row_width % A_WIDTH == 0
    rows = _tile(sub, (4 * blk, 2 * blk, blk))
    nb = sub // rows
    rpb = rows // blk
    cpr = row_width // A_WIDTH
    t = batch * seq
    view = qkv.reshape(t // dilation, dilation * row_width)

    def cur(c):
        return pl.BlockSpec((rows, A_WIDTH), lambda b, r, n: (b * nb + n, cpr * r + c))

    def prev(c):
        return pl.BlockSpec((blk, A_WIDTH),
                            lambda b, r, n: (b * (sub // blk) + jnp.maximum(n * rpb - 1, 0), cpr * r + c))

    o, lse = pl.pallas_call(
        functools.partial(_dilated_kernel, rows=rows, blk=blk),
        name=f"dilated_d{dilation}",
        grid=(batch, dilation, nb),
        in_specs=[cur(0), prev(1), cur(1), prev(2), cur(2)],
        out_specs=[pl.BlockSpec((rows, A_WIDTH), lambda b, r, n: (b * nb + n, r)),
                   pl.BlockSpec((rows, LANES), lambda b, r, n: (b * nb + n, r))],
        out_shape=[jax.ShapeDtypeStruct((t // dilation, dilation * A_WIDTH), BF16),
                   jax.ShapeDtypeStruct((t // dilation, dilation * LANES), F32)],
        scratch_shapes=[pltpu.VMEM((rows + blk, A_WIDTH), BF16), pltpu.VMEM((rows + blk, A_WIDTH), BF16)],
        compiler_params=_cparams(("arbitrary", "arbitrary", "arbitrary"),
                                 2 * (3 * rows + 2 * blk) * A_WIDTH * 2 + 2 * rows * (A_WIDTH + LANES) * 4
                                 + 2 * (rows + blk) * A_WIDTH * 2 + 16 * blk * 2 * blk * 4),
    )(view, view, view, view, view)
    return o.reshape(t, A_WIDTH), lse.reshape(t, LANES)


DEINTERLEAVE_ROWS = 16 * max(d for _, d in A_PATTERNS)


def _deinterleave_kernel(x_ref, *refs, dilations):
    out_refs, scr = refs[:-1], refs[-1]
    rows = scr.shape[1]
    for g in range(scr.shape[0]):
        cols = slice(g * LANES, (g + 1) * LANES)
        scr[g] = x_ref[:, cols].astype(F32)
        for o_ref, d in zip(out_refs, dilations):
            for r in range(d):
                o_ref[r, :, cols] = scr[g, pl.ds(r, rows // d, stride=d), :].astype(o_ref.dtype)


def _deinterleave(x, batch, seq, dilations):
    t, w = x.shape
    rows = DEINTERLEAVE_ROWS
    nb = seq // rows
    outs = pl.pallas_call(
        functools.partial(_deinterleave_kernel, dilations=dilations),
        name="dilated_deinterleave",
        grid=(batch, nb),
        in_specs=[pl.BlockSpec((rows, w), lambda b, n: (b * nb + n, 0))],
        out_specs=[pl.BlockSpec((None, d, rows // d, w), lambda b, n: (b, 0, n, 0)) for d in dilations],
        out_shape=[jax.ShapeDtypeStruct((batch, d, seq // d, w), x.dtype) for d in dilations],
        scratch_shapes=[pltpu.VMEM((w // LANES, rows, LANES), F32)],
        compiler_params=_cparams(("arbitrary", "arbitrary"), rows * w * (4 + 4 + 4 * len(dilations) + 8)),
    )(x)
    return [o.reshape(t, w) for o in outs]


def _combine_kernel(*refs, dilations):
    n = len(dilations)
    o_refs, l_refs, out_ref, scratch = refs[:n], refs[n:2 * n], refs[2 * n], refs[2 * n + 1:]
    heads, lses = [], []
    for o_ref, l_ref, d in zip(o_refs, l_refs, dilations):
        if d == 1:
            heads.append(lambda hd, o_ref=o_ref: o_ref[:, hd * HEAD_DIM:(hd + 1) * HEAD_DIM])
            lses.append(l_ref[...])
            continue
        o_sc, l_sc = scratch[:2]
        scratch = scratch[2:]
        rows = o_sc.shape[1]
        for r in range(d):
            dst = pl.ds(r, rows // d, stride=d)
            l_sc[0, dst, :] = l_ref[r]
            for hd in range(A_HEADS):
                o_sc[hd, dst, :] = o_ref[r, :, hd * HEAD_DIM:(hd + 1) * HEAD_DIM].astype(F32)
        heads.append(lambda hd, o_sc=o_sc: o_sc[hd])
        lses.append(l_sc[0])
    mx = functools.reduce(jnp.maximum, lses)
    es = [jnp.exp2(a - mx) for a in lses]
    inv = 1.0 / functools.reduce(jnp.add, es)
    ws = [e * inv for e in es]
    for hd in range(A_HEADS):
        acc = functools.reduce(jnp.add, [w[:, hd:hd + 1] * head(hd) for w, head in zip(ws, heads)])
        out_ref[:, hd * HEAD_DIM:(hd + 1) * HEAD_DIM] = acc.astype(out_ref.dtype)


def _combine(outs, lses, batch, seq, dilations):
    t = batch * seq
    rows = DEINTERLEAVE_ROWS
    nb = seq // rows
    in_specs, args, scratch = [], [], []
    for width, arrs in ((A_WIDTH, outs), (LANES, lses)):
        for a, d in zip(arrs, dilations):
            if d == 1:
                in_specs.append(pl.BlockSpec((rows, width), lambda b, n: (b * nb + n, 0)))
                args.append(a)
            else:
                in_specs.append(pl.BlockSpec((None, d, rows // d, width), lambda b, n: (b, 0, n, 0)))
                args.append(a.reshape(batch, d, seq // d, width))
    for d in dilations:
        if d != 1:
            scratch += [pltpu.VMEM((A_HEADS, rows, LANES), F32), pltpu.VMEM((1, rows, LANES), F32)]
    return pl.pallas_call(
        functools.partial(_combine_kernel, dilations=dilations),
        name="dilated_combine",
        grid=(batch, nb),
        in_specs=in_specs,
        out_specs=pl.BlockSpec((rows, A_WIDTH), lambda b, n: (b * nb + n, 0)),
        out_shape=jax.ShapeDtypeStruct((t, A_WIDTH), BF16),
        scratch_shapes=scratch,
        compiler_params=_cparams(("arbitrary", "arbitrary"),
                                 rows * (A_WIDTH + LANES) * 4 * (3 * len(dilations) + 4)),
    )(*args)


def _merge_kernel(h_ref, oa_ref, ob_ref, oc_ref, wg0, wg1, wg2, wa, wb, wc, o_ref):
    h = h_ref[...]
    acc = jax.nn.sigmoid(_dot(h, wg0[...])) * _dot(oa_ref[...], wa[...])
    acc += jax.nn.sigmoid(_dot(h, wg1[...])) * _dot(ob_ref[...], wb[...])
    acc += jax.nn.sigmoid(_dot(h, wg2[...])) * _dot(oc_ref[...], wc[...])
    o_ref[...] = acc.astype(o_ref.dtype)


def _merge(h, oa, ob, oc, w_gate, w_a, w_b, w_c):
    m, d = h.shape
    bm = _tile(m, (512, 256, 128))
    bn = _tile(d, (512, 256, 128))
    nj = d // bn

    def rows(width):
        return pl.BlockSpec((bm, width), lambda i, j: (i, 0))

    def gate(g):
        return pl.BlockSpec((d, bn), lambda i, j: (0, g * nj + j))

    def branch(width):
        return pl.BlockSpec((width, bn), lambda i, j: (0, j))

    k_all = 3 * d + A_WIDTH + B_WIDTH + C_WIDTH
    return pl.pallas_call(
        _merge_kernel,
        name="gate_merge",
        grid=(m // bm, nj),
        in_specs=[rows(d), rows(A_WIDTH), rows(B_WIDTH), rows(C_WIDTH), gate(0), gate(1), gate(2),
                  branch(A_WIDTH), branch(B_WIDTH), branch(C_WIDTH)],
        out_specs=pl.BlockSpec((bm, bn), lambda i, j: (i, j)),
        out_shape=jax.ShapeDtypeStruct((m, d), BF16),
        compiler_params=_cparams(("arbitrary", "arbitrary"),
                                 2 * (bm * (d + A_WIDTH + B_WIDTH + C_WIDTH) * 2 + k_all * bn * 2 + bm * bn * 2)
                                 + 8 * bm * bn * 4),
    )(h, oa, ob, oc, w_gate, w_gate, w_gate, w_a, w_b, w_c)


def _ffn_in_kernel(h_ref, wa_ref, wb_ref, o_ref):
    h = h_ref[...]
    a = _dot(h, wa_ref[...])
    b = _dot(h, wb_ref[...])
    o_ref[...] = (a * jax.nn.sigmoid(a) * b).astype(o_ref.dtype)


def _ffn_in(h, w_in, d_ff):
    m, d = h.shape
    bm = _tile(m, (1024, 512, 256, 128))
    bn = _tile(d_ff, (512, 256, 128))
    nj = d_ff // bn
    return pl.pallas_call(
        _ffn_in_kernel,
        name="ffn_in",
        grid=(m // bm, nj),
        in_specs=[pl.BlockSpec((bm, d), lambda i, j: (i, 0)),
                  pl.BlockSpec((d, bn), lambda i, j: (0, j)),
                  pl.BlockSpec((d, bn), lambda i, j: (0, nj + j))],
        out_specs=pl.BlockSpec((bm, bn), lambda i, j: (i, j)),
        out_shape=jax.ShapeDtypeStruct((m, d_ff), BF16),
        compiler_params=_cparams(("arbitrary", "arbitrary"),
                                 2 * (bm * d * 2 + 2 * d * bn * 2 + bm * bn * 2) + 4 * bm * bn * 4),
    )(h, w_in, w_in)


def _layer(x2, h, ada_l, ada_next, batch, seq, tabs_a, tabs_b, w_in_bf, l, w, alpha, last):
    d = x2.shape[1]
    t = batch * seq
    sh1, sc1, gt1, sh2, sc2, gt2 = [a.reshape(batch, 1, d) for a in jnp.split(ada_l, 6, axis=-1)]
    del sh1, sc1

    qkv_a = _proj_ac(h, w_in_bf, l, SCALE_A, tabs_a)
    qkv_c = _matmul(h, w["c"], BF16, bn_prefs=(768, 512, 256, 128), scale=SCALE_A, scaled_cols=C_WIDTH)
    q_b = _mla_q(h, w_in_bf, l, w["cq_col0"], w["g_qn"], w["uq"], tabs_b)
    k_b, v_b, f_logit = _mla_kv(h, w["ckv"], w["g_kvn"], w["ukv"], w["b_f"], tabs_b)

    dilations = tuple(dil for _, dil in A_PATTERNS)
    strided = [dil for dil in dilations if dil != 1]
    copies = dict(zip(strided, _deinterleave(qkv_a, batch, seq, strided)))
    outs, lses = [], []
    for window, dil in A_PATTERNS:
        src = qkv_a if dil == 1 else copies[dil]
        o, l = _dilated_group(src, src.shape[1], batch * dil, seq // dil, window // dil, 1)
        outs.append(o)
        lses.append(l)
    o_a = _combine(outs, lses, batch, seq, dilations)

    o_b = _flash(q_b, k_b, v_b, 0, 0, 0, B_QK_PAD, V_DIM, B_HEADS, batch, seq)

    f_t = f_logit[:, :F_ROWS].reshape(batch, seq, F_ROWS).transpose(0, 2, 1)
    c_t = _fox_cumsum(f_t)
    c_row = c_t.reshape(batch * F_ROWS, 1, seq)
    c_rep = jnp.broadcast_to(c_t[:, :C_HEADS, :, None], (batch, C_HEADS, seq, LANES))
    c_rep = c_rep.reshape(batch * C_HEADS, seq, LANES)
    nh = A_WIDTH // HEAD_DIM
    o_c = _flash(qkv_c, qkv_c, qkv_c, 0, nh, 2 * nh, HEAD_DIM, HEAD_DIM, C_HEADS, batch, seq,
                 bias=(c_row, c_rep))

    merged = _merge(h, o_a, o_b, o_c, w["gate"], w["br_a"], w["br_b"], w["br_c"])
    y = _matmul(merged, w["o"], BF16)
    x2, h2 = _res_ln(x2, y, gt1, w["ln1_g"], w["ln1_b"], sc2, sh2, batch, alpha, True)

    act = _ffn_in(h2, w["ffn_in"], w["d_ff"])
    y = _matmul(act, w["ffn_out"], BF16, bm_prefs=(512, 256, 128), bn_prefs=(512, 256, 128))
    if last:
        x2, hn = _res_ln(x2, y, gt2, w["ln2_g"], w["ln2_b"], sc2, sh2, batch, alpha, False)
    else:
        sh1n, sc1n = [a.reshape(batch, 1, d) for a in jnp.split(ada_next, 6, axis=-1)[:2]]
        x2, hn = _res_ln(x2, y, gt2, w["ln2_g"], w["ln2_b"], sc1n, sh1n, batch, alpha, True)
    return x2, hn


SCALE_A = HEAD_DIM ** -0.5 * LOG2E
SCALE_B = (QK_NOPE + QK_ROPE) ** -0.5 * LOG2E


def _prep_weights(l, w_in_bf, b_f, g_qn, w_uq, g_kvn, w_ukv, w_br_a, w_br_b, w_br_c, w_o,
                  ln1_g, ln1_b, w_ffn_in, w_ffn_out, ln2_g, ln2_b):
    d = w_o.shape[1]
    ql = g_qn.shape[1]
    kvl = g_kvn.shape[1]
    wi = w_in_bf[l * d:(l + 1) * d]
    widths = (A_WIDTH, A_WIDTH, A_WIDTH, ql, kvl, QK_ROPE, C_WIDTH, C_WIDTH, C_WIDTH, C_HEADS, N_BRANCH * d)
    offs = [int(o) for o in np.concatenate([[0], np.cumsum(widths)])]
    ckv, kr, fl = [wi[:, offs[i]:offs[i + 1]] for i in (4, 5, 9)]
    zpad = lambda n: jnp.zeros((d, n), BF16)
    w_ckv = jnp.concatenate([ckv, kr, zpad(LANES - QK_ROPE), fl, zpad(LANES - C_HEADS)], axis=1)
    uq = w_uq[l].reshape(ql, B_HEADS, QK_NOPE + QK_ROPE) * SCALE_B
    uq = jnp.pad(uq, ((0, 0), (0, 0), (0, B_QK_PAD - QK_NOPE - QK_ROPE))).reshape(ql, B_HEADS * B_QK_PAD)
    ukv = w_ukv[l].reshape(kvl, B_HEADS, 2, QK_NOPE).transpose(0, 2, 1, 3).reshape(kvl, 2 * B_HEADS * QK_NOPE)
    b_f_row = jnp.pad(b_f[l], (0, LANES - C_HEADS)).reshape(1, LANES)
    return dict(cq_col0=offs[3], c=wi[:, offs[6]:offs[9]], g_qn=g_qn[l], uq=uq.astype(BF16), ckv=w_ckv,
                g_kvn=g_kvn[l], ukv=ukv.astype(BF16), b_f=b_f_row, gate=wi[:, offs[10]:],
                br_a=w_br_a[l].astype(BF16), br_b=w_br_b[l].astype(BF16), br_c=w_br_c[l].astype(BF16),
                o=w_o[l].astype(BF16), ln1_g=ln1_g[l], ln1_b=ln1_b[l], ffn_in=w_ffn_in[l].astype(BF16),
                ffn_out=w_ffn_out[l].astype(BF16), d_ff=w_ffn_out.shape[1], ln2_g=ln2_g[l], ln2_b=ln2_b[l])


def kernel(x, c, positions, w_ada, b_ada, w_in, b_f, g_qn, w_uq, g_kvn, w_ukv, w_br_a, w_br_b, w_br_c, w_o,
           ln1_g, ln1_b, w_ffn_in, w_ffn_out, ln2_g, ln2_b):
    batch, seq, d = x.shape
    depth = w_ada.shape[0]
    alpha = (2.0 * depth) ** 0.25
    t = batch * seq
    x2 = x.reshape(t, d)
    ada = _ada(c, w_ada, b_ada)
    pos_col = positions.astype(F32).reshape(t, 1)
    tabs_a = _rope_tables(pos_col, PARTIAL_ROPE_DIM)
    tabs_b = _rope_tables(pos_col, QK_ROPE)
    sh1, sc1 = [a.reshape(batch, 1, d) for a in jnp.split(ada[0], 6, axis=-1)[:2]]
    h = _modulate(x2, sc1, sh1, batch)
    w_in_bf = w_in.reshape(depth * d, w_in.shape[2]).astype(BF16)
    for l in range(depth):
        w = _prep_weights(l, w_in_bf, b_f, g_qn, w_uq, g_kvn, w_ukv, w_br_a, w_br_b, w_br_c, w_o,
                          ln1_g, ln1_b, w_ffn_in, w_ffn_out, ln2_g, ln2_b)
        last = l == depth - 1
        x2, h = _layer(x2, h, ada[l], None if last else ada[l + 1], batch, seq, tabs_a, tabs_b, w_in_bf, l, w,
                       alpha, last)
    return x2.reshape(batch, seq, d)
```

```python
import functools
import math

import jax
import jax.numpy as jnp
import numpy as np
from jax import lax
from jax.experimental import pallas as pl
from jax.experimental.pallas import tpu as pltpu

HEAD_DIM = 128
ROPE_THETA = 500000.0
PARTIAL_ROPE_DIM = HEAD_DIM // 4
A_HEADS = 12
A_PATTERNS = ((128, 1), (512, 4), (2048, 16))
B_HEADS = 8
QK_NOPE = 128
QK_ROPE = 64
V_DIM = 128
C_HEADS = 12
N_BRANCH = 3
A_WIDTH = A_HEADS * HEAD_DIM
B_WIDTH = B_HEADS * V_DIM
C_WIDTH = C_HEADS * HEAD_DIM
B_QK_PAD = 256
F_ROWS = 16
FLASH_BQ = 4096
FLASH_BK = 1024
NEG = -1e30
LOG2E = math.log2(math.e)
LANES = 128
V7X_VMEM_CAP = 60 * 1024 * 1024

BF16 = jnp.bfloat16
F32 = jnp.float32


def _cparams(semantics, vmem_estimate):
    limit = int(min(max(vmem_estimate * 5 // 4, 32 * 1024 * 1024), V7X_VMEM_CAP))
    return pltpu.CompilerParams(dimension_semantics=semantics, vmem_limit_bytes=limit)


def _tile(n, prefs):
    for p in prefs:
        if n % p == 0:
            return p
    return n


def _resident(shape):
    return pl.BlockSpec(shape, lambda i: (0,) * len(shape), pipeline_mode=pl.Buffered(1))


def _dot(a, b):
    return jnp.dot(a, b, preferred_element_type=F32)


def _dot_nt(a, b):
    return lax.dot_general(a, b, (((1,), (1,)), ((), ())), preferred_element_type=F32)


def _rotate(t, cos, sin_lo, sin_hi, half):
    return t * cos + pltpu.roll(t, half, 1) * sin_hi + pltpu.roll(t, LANES - half, 1) * sin_lo


def _ada_kernel(c_ref, w_ref, b_ref, o_ref, acc_sc):
    kblk = pl.program_id(1)
    nb, kb = c_ref.shape[0], c_ref.shape[1]

    @pl.when(kblk == 0)
    def _():
        acc_sc[...] = jnp.zeros_like(acc_sc)

    cv = c_ref[...]
    s = cv * jax.nn.sigmoid(cv)
    for g in range(w_ref.shape[1] // LANES):
        cols = slice(g * LANES, (g + 1) * LANES)
        wg = w_ref[:, cols]
        for b in range(nb):
            acc_sc[b, :, cols] += jnp.sum((wg * s[b]).reshape(kb // 8, 8, LANES), axis=0)

    @pl.when(kblk == pl.num_programs(1) - 1)
    def _():
        for b in range(nb):
            o_ref[b:b + 1, :] = jnp.sum(acc_sc[b], axis=0, keepdims=True) + b_ref[...]


def _ada(c, w_ada, b_ada):
    depth, d, n = w_ada.shape
    b = c.shape[0]
    kb = _tile(d, (128, 64, 32, 16, 8))
    c_rep = jnp.broadcast_to(c[:, :, None], (b, d, LANES))
    return pl.pallas_call(
        _ada_kernel,
        name="ada",
        grid=(depth, d // kb),
        in_specs=[pl.BlockSpec((b, kb, LANES), lambda l, k: (0, k, 0)),
                  pl.BlockSpec((None, kb, n), lambda l, k: (l, k, 0)),
                  pl.BlockSpec((None, 1, n), lambda l, k: (l, 0, 0))],
        out_specs=pl.BlockSpec((None, b, n), lambda l, k: (l, 0, 0)),
        out_shape=jax.ShapeDtypeStruct((depth, b, n), F32),
        scratch_shapes=[pltpu.VMEM((b, 8, n), F32)],
        compiler_params=_cparams(("arbitrary", "arbitrary"), 2 * kb * n * 4 + 4 * b * 8 * n * 4),
    )(c_rep, w_ada, b_ada.reshape(depth, 1, n))


def _rope_table_kernel(pos_ref, freq_ref, mc_ref, m1_ref, mlo_ref, mhi_ref, cos_ref, lo_ref, hi_ref):
    ang = pos_ref[...] * freq_ref[...]
    cs = jnp.cos(ang)
    sn = jnp.sin(ang)
    cos_ref[...] = cs * mc_ref[...] + m1_ref[...]
    lo_ref[...] = sn * mlo_ref[...]
    hi_ref[...] = sn * mhi_ref[...]


def _rope_tables(pos_col, rot_dim):
    t = pos_col.shape[0]
    half = rot_dim // 2
    inv = np.exp(-math.log(ROPE_THETA) * np.arange(half, dtype=np.float32) * np.float32(2.0 / rot_dim))
    lane = np.arange(LANES)
    freq = np.where(lane < rot_dim, inv[lane % half], 0.0).astype(np.float32)[None]
    m_cos = (lane < rot_dim).astype(np.float32)[None]
    m_one = (lane >= rot_dim).astype(np.float32)[None]
    m_lo = np.where(lane < half, -1.0, 0.0).astype(np.float32)[None]
    m_hi = np.where((lane >= half) & (lane < rot_dim), 1.0, 0.0).astype(np.float32)[None]
    bm = _tile(t, (2048, 1024, 512, 256, 128))
    row = pl.BlockSpec((1, LANES), lambda i: (0, 0))
    tab = pl.BlockSpec((bm, LANES), lambda i: (i, 0))
    shp = jax.ShapeDtypeStruct((t, LANES), F32)
    return pl.pallas_call(
        _rope_table_kernel,
        name="rope_tables",
        grid=(t // bm,),
        in_specs=[pl.BlockSpec((bm, 1), lambda i: (i, 0)), row, row, row, row, row],
        out_specs=[tab, tab, tab],
        out_shape=[shp, shp, shp],
        compiler_params=_cparams(("arbitrary",), 16 * bm * LANES * 4),
    )(pos_col, jnp.asarray(freq), jnp.asarray(m_cos), jnp.asarray(m_one), jnp.asarray(m_lo), jnp.asarray(m_hi))


def _mod_kernel(x_ref, sc_ref, sh_ref, o_ref):
    o_ref[...] = (x_ref[...] * (1.0 + sc_ref[...]) + sh_ref[...]).astype(o_ref.dtype)


def _modulate(x2, sc, sh, batch):
    t, d = x2.shape
    s = t // batch
    bm = _tile(s, (512, 256, 128))
    nb = s // bm
    vec = pl.BlockSpec((None, 1, d), lambda b, i: (b, 0, 0))
    return pl.pallas_call(
        _mod_kernel,
        name="modulate",
        grid=(batch, nb),
        in_specs=[pl.BlockSpec((bm, d), lambda b, i: (b * nb + i, 0)), vec, vec],
        out_specs=pl.BlockSpec((bm, d), lambda b, i: (b * nb + i, 0)),
        out_shape=jax.ShapeDtypeStruct((t, d), BF16),
        compiler_params=_cparams(("arbitrary", "arbitrary"), 2 * bm * d * 6),
    )(x2, sc, sh)


def _res_ln_kernel(x_ref, y_ref, gt_ref, g_ref, b_ref, sc_ref, sh_ref, xo_ref, *ho_ref, alpha):
    z = alpha * x_ref[...] + (1.0 + gt_ref[...]) * y_ref[...].astype(F32)
    mu = jnp.mean(z, axis=-1, keepdims=True)
    zc = z - mu
    var = jnp.mean(zc * zc, axis=-1, keepdims=True)
    xn = zc * lax.rsqrt(var + 1e-5) * g_ref[...] + b_ref[...]
    xo_ref[...] = xn
    if ho_ref:
        ho_ref[0][...] = (xn * (1.0 + sc_ref[...]) + sh_ref[...]).astype(BF16)


def _res_ln(x2, y2, gt, ln_g, ln_b, sc, sh, batch, alpha, with_h):
    t, d = x2.shape
    s = t // batch
    bm = _tile(s, (256, 128))
    nb = s // bm
    vec = pl.BlockSpec((None, 1, d), lambda b, i: (b, 0, 0))
    par = pl.BlockSpec((1, d), lambda b, i: (0, 0))
    blk = pl.BlockSpec((bm, d), lambda b, i: (b * nb + i, 0))
    out_specs = [blk, blk] if with_h else [blk]
    out_shape = [jax.ShapeDtypeStruct((t, d), F32)]
    if with_h:
        out_shape.append(jax.ShapeDtypeStruct((t, d), BF16))
    outs = pl.pallas_call(
        functools.partial(_res_ln_kernel, alpha=alpha),
        name="res_ln",
        grid=(batch, nb),
        in_specs=[blk, blk, vec, par, par, vec, vec],
        out_specs=out_specs,
        out_shape=out_shape,
        compiler_params=_cparams(("arbitrary", "arbitrary"), 2 * bm * d * 14 + 6 * bm * d * 4),
    )(x2, y2, gt, ln_g.reshape(1, d), ln_b.reshape(1, d), sc, sh)
    return (outs[0], outs[1]) if with_h else (outs[0], None)


def _mm_kernel(x_ref, w_ref, o_ref, *, scale, n_scaled):
    acc = _dot(x_ref[...], w_ref[...])
    if n_scaled:
        acc = acc * jnp.where(pl.program_id(1) < n_scaled, scale, 1.0)
    o_ref[...] = acc.astype(o_ref.dtype)


def _matmul(x, w, out_dtype, bm_prefs=(1024, 512, 256, 128), bn_prefs=(1024, 512, 256, 128), scale=1.0,
            scaled_cols=0):
    m, k = x.shape
    n = w.shape[1]
    bm = _tile(m, bm_prefs)
    bn = _tile(n, bn_prefs)
    assert scaled_cols % bn == 0
    osz = jnp.dtype(out_dtype).itemsize
    return pl.pallas_call(
        functools.partial(_mm_kernel, scale=scale, n_scaled=scaled_cols // bn),
        name="matmul",
        grid=(m // bm, n // bn),
        in_specs=[pl.BlockSpec((bm, k), lambda i, j: (i, 0)),
                  pl.BlockSpec((k, bn), lambda i, j: (0, j))],
        out_specs=pl.BlockSpec((bm, bn), lambda i, j: (i, j)),
        out_shape=jax.ShapeDtypeStruct((m, n), out_dtype),
        compiler_params=_cparams(("arbitrary", "arbitrary"),
                                 2 * (bm * k * 2 + k * bn * 2 + bm * bn * osz) + bm * bn * 4),
    )(x, w)


def _proj_ac_kernel(x_ref, w_ref, cos_ref, lo_ref, hi_ref, o_ref, *, n_rot_tiles, half, scale):
    j = pl.program_id(1)
    acc = _dot(x_ref[...], w_ref[...]) * jnp.where(j < n_rot_tiles // 2, scale, 1.0)

    @pl.when(j < n_rot_tiles)
    def _():
        cs, lo, hi = cos_ref[...], lo_ref[...], hi_ref[...]
        for g in range(acc.shape[1] // LANES):
            sl = slice(g * LANES, (g + 1) * LANES)
            o_ref[:, sl] = _rotate(acc[:, sl], cs, lo, hi, half).astype(o_ref.dtype)

    @pl.when(j >= n_rot_tiles)
    def _():
        o_ref[...] = acc.astype(o_ref.dtype)


def _proj_ac(h, w_in_bf, l, scale, tabs):
    m, k = h.shape
    n = 3 * A_WIDTH
    bm = _tile(m, (1024, 512, 256, 128))
    bn = _tile(A_WIDTH, (768, 512, 256, 128))
    tab = pl.BlockSpec((bm, LANES), lambda i, j: (i, 0))
    return pl.pallas_call(
        functools.partial(_proj_ac_kernel, n_rot_tiles=2 * A_WIDTH // bn, half=PARTIAL_ROPE_DIM // 2, scale=scale),
        name="proj_ac",
        grid=(m // bm, n // bn),
        in_specs=[pl.BlockSpec((bm, k), lambda i, j: (i, 0)),
                  pl.BlockSpec((k, bn), lambda i, j: (l, j)), tab, tab, tab],
        out_specs=pl.BlockSpec((bm, bn), lambda i, j: (i, j)),
        out_shape=jax.ShapeDtypeStruct((m, n), BF16),
        compiler_params=_cparams(("arbitrary", "arbitrary"),
                                 2 * (bm * k * 2 + k * bn * 2 + bm * bn * 2 + 3 * bm * LANES * 4) + 2 * bm * bn * 4),
    )(h, w_in_bf, *tabs)


def _rms(x, g, eps=1e-6):
    return x * lax.rsqrt(jnp.mean(x * x, axis=-1, keepdims=True) + eps) * g


def _mla_q_kernel(h_ref, wcq_ref, g_ref, wuq_ref, cos_ref, lo_ref, hi_ref, o_ref):
    cq = _dot(h_ref[...], wcq_ref[...])
    q = _dot(_rms(cq, g_ref[...]).astype(BF16), wuq_ref[...])
    cs, lo, hi = cos_ref[...], lo_ref[...], hi_ref[...]
    for hd in range(B_HEADS):
        base = hd * B_QK_PAD
        o_ref[:, base:base + QK_NOPE] = q[:, base:base + QK_NOPE].astype(BF16)
        rope = _rotate(q[:, base + QK_NOPE:base + B_QK_PAD], cs, lo, hi, QK_ROPE // 2)
        o_ref[:, base + QK_NOPE:base + B_QK_PAD] = rope.astype(BF16)


def _mla_q(h, w_in_bf, l, cq_col0, g_qn, w_uq, tabs):
    m, k = h.shape
    ql = g_qn.shape[0]
    n = w_uq.shape[1]
    assert cq_col0 % ql == 0
    bm = _tile(m, (512, 256, 128))
    tab = pl.BlockSpec((bm, LANES), lambda i: (i, 0))
    return pl.pallas_call(
        _mla_q_kernel,
        name="mla_q",
        grid=(m // bm,),
        in_specs=[pl.BlockSpec((bm, k), lambda i: (i, 0)),
                  pl.BlockSpec((k, ql), lambda i: (l, cq_col0 // ql), pipeline_mode=pl.Buffered(1)),
                  _resident((1, ql)), _resident((ql, n)), tab, tab, tab],
        out_specs=pl.BlockSpec((bm, n), lambda i: (i, 0)),
        out_shape=jax.ShapeDtypeStruct((m, n), BF16),
        compiler_params=_cparams(("arbitrary",),
                                 2 * (bm * k * 2 + bm * n * 2) + k * ql * 2 + ql * n * 2 + bm * (ql + n) * 8),
    )(h, w_in_bf, g_qn.reshape(1, ql), w_uq, *tabs)


def _mla_kv_kernel(h_ref, wc_ref, g_ref, wukv_ref, bf_ref, cos_ref, lo_ref, hi_ref, k_ref, v_ref, f_ref, *, kvl):
    ck = _dot(h_ref[...], wc_ref[...])
    kv = _dot(_rms(ck[:, :kvl], g_ref[...]).astype(BF16), wukv_ref[...])
    kr = _rotate(ck[:, kvl:kvl + LANES], cos_ref[...], lo_ref[...], hi_ref[...], QK_ROPE // 2).astype(BF16)
    for hd in range(B_HEADS):
        base = hd * B_QK_PAD
        k_ref[:, base:base + QK_NOPE] = kv[:, hd * QK_NOPE:(hd + 1) * QK_NOPE].astype(BF16)
        k_ref[:, base + QK_NOPE:base + B_QK_PAD] = kr
    v_ref[...] = kv[:, B_HEADS * QK_NOPE:].astype(BF16)
    f_ref[...] = ck[:, kvl + LANES:] + bf_ref[...]


def _mla_kv(h, w_c, g_kvn, w_ukv, b_f_row, tabs):
    m, k = h.shape
    kvl = g_kvn.shape[0]
    nc = w_c.shape[1]
    bm = _tile(m, (512, 256, 128))
    tab = pl.BlockSpec((bm, LANES), lambda i: (i, 0))
    nk = B_HEADS * B_QK_PAD
    return pl.pallas_call(
        functools.partial(_mla_kv_kernel, kvl=kvl),
        name="mla_kv",
        grid=(m // bm,),
        in_specs=[pl.BlockSpec((bm, k), lambda i: (i, 0)),
                  _resident((k, nc)), _resident((1, kvl)), _resident(w_ukv.shape), _resident((1, LANES)),
                  tab, tab, tab],
        out_specs=[pl.BlockSpec((bm, nk), lambda i: (i, 0)),
                   pl.BlockSpec((bm, B_WIDTH), lambda i: (i, 0)),
                   pl.BlockSpec((bm, LANES), lambda i: (i, 0))],
        out_shape=[jax.ShapeDtypeStruct((m, nk), BF16),
                   jax.ShapeDtypeStruct((m, B_WIDTH), BF16),
                   jax.ShapeDtypeStruct((m, LANES), F32)],
        compiler_params=_cparams(("arbitrary",),
                                 2 * (bm * k * 2 + k * nc * 2 + w_ukv.size * 2 + bm * (nk + B_WIDTH) * 2)
                                 + bm * (nc + nk + B_WIDTH) * 8),
    )(h, w_c, g_kvn.reshape(1, kvl), w_ukv, b_f_row, *tabs)


def _fox_cumsum_kernel(f_ref, o_ref):
    x = f_ref[...]
    y = (jnp.minimum(x, 0.0) - jnp.log(1.0 + jnp.exp(-jnp.abs(x)))) * LOG2E
    s = y.shape[1]
    lane = lax.broadcasted_iota(jnp.int32, y.shape, 1)
    shift = 1
    while shift < s:
        y = y + jnp.where(lane >= shift, pltpu.roll(y, shift, 1), 0.0)
        shift *= 2
    o_ref[...] = y


def _fox_cumsum(f_t):
    b, r, s = f_t.shape
    return pl.pallas_call(
        _fox_cumsum_kernel,
        name="fox_cumsum",
        grid=(b,),
        in_specs=[pl.BlockSpec((None, r, s), lambda i: (i, 0, 0))],
        out_specs=pl.BlockSpec((None, r, s), lambda i: (i, 0, 0)),
        out_shape=jax.ShapeDtypeStruct((b, r, s), F32),
        compiler_params=_cparams(("arbitrary",), 8 * r * s * 4),
    )(f_t)


def _flash_kernel(*refs, bq, bk, has_bias):
    if has_bias:
        q_ref, k_ref, v_ref, cq_ref, ck_ref, o_ref, acc_sc = refs
    else:
        q_ref, k_ref, v_ref, o_ref, acc_sc = refs
    n_diag = bq // bk
    n_full = pl.program_id(2) * n_diag

    def step(j, m, l, q0, masked):
        nq = bq - q0
        start = pl.multiple_of(j * bk, bk)
        s = _dot_nt(k_ref[pl.ds(start, bk), :], q_ref[q0:, :])
        if has_bias:
            ck = ck_ref[pl.ds(start, bk), :]
            s = s + cq_ref[:, q0:] - jnp.concatenate([ck] * (nq // LANES), axis=1)
        if masked:
            key = lax.broadcasted_iota(jnp.int32, (bk, nq), 0)
            qry = lax.broadcasted_iota(jnp.int32, (bk, nq), 1)
            s = jnp.where(key <= qry, s, NEG)
        m_prev = m[:, q0:]
        m_new = jnp.maximum(m_prev, jnp.max(s, axis=0, keepdims=True))
        alpha = jnp.exp2(m_prev - m_new)
        p = jnp.exp2(s - m_new)
        l_new = alpha * l[:, q0:] + jnp.sum(p, axis=0, keepdims=True)
        pv = lax.dot_general(v_ref[pl.ds(start, bk), :], p.astype(BF16), (((0,), (0,)), ((), ())),
                             preferred_element_type=F32)
        acc_sc[:, q0:] = alpha * acc_sc[:, q0:] + pv
        if q0:
            m_new = jnp.concatenate([m[:, :q0], m_new], axis=1)
            l_new = jnp.concatenate([l[:, :q0], l_new], axis=1)
        return m_new, l_new

    acc_sc[...] = jnp.zeros_like(acc_sc)
    init = (jnp.full((1, bq), NEG, F32), jnp.zeros((1, bq), F32))
    m, l = lax.fori_loop(0, n_full, lambda j, c: step(j, c[0], c[1], 0, False), init)
    for t in range(n_diag):
        m, l = step(n_full + t, m, l, t * bk, True)
    o_ref[...] = jnp.transpose(acc_sc[...] / l).astype(o_ref.dtype)


def _flash(q_arr, k_arr, v_arr, q_col0, k_col0, v_col0, dq, dv, heads, batch, seq, bias=None):
    bq = _tile(seq, (FLASH_BQ, 2048, 1024, 512, 256, 128))
    bk = _tile(bq, (FLASH_BK, 512, 256, 128))
    nq = seq // bq
    t = batch * seq
    in_specs = [pl.BlockSpec((bq, dq), lambda b, h, i: (b * nq + i, q_col0 + h)),
                pl.BlockSpec((seq, dq), lambda b, h, i: (b, k_col0 + h)),
                pl.BlockSpec((seq, dv), lambda b, h, i: (b, v_col0 + h))]
    args = [q_arr, k_arr, v_arr]
    if bias is not None:
        c_row, c_rep = bias
        in_specs += [pl.BlockSpec((None, 1, bq), lambda b, h, i: (b * F_ROWS + h, 0, i)),
                     pl.BlockSpec((None, seq, LANES), lambda b, h, i: (b * heads + h, 0, 0))]
        args += [c_row, c_rep]
    return pl.pallas_call(
        functools.partial(_flash_kernel, bq=bq, bk=bk, has_bias=bias is not None),
        name="flash_fox" if bias is not None else "flash_mla",
        grid=(batch, heads, nq),
        in_specs=in_specs,
        out_specs=pl.BlockSpec((bq, dv), lambda b, h, i: (b * nq + i, h)),
        out_shape=jax.ShapeDtypeStruct((t, heads * dv), BF16),
        scratch_shapes=[pltpu.VMEM((dv, bq), F32)],
        compiler_params=_cparams(("arbitrary", "arbitrary", "arbitrary"),
                                 2 * (seq * (dq + dv) * 2 + bq * (dq + dv) * 2 + seq * LANES * 4)
                                 + 4 * bq * bk * 4),
    )(*args)


def _dilated_kernel(q_ref, kp_ref, kc_ref, vp_ref, vc_ref, o_ref, lse_ref, kband, vband, *, rows, blk):
    n = pl.program_id(2)
    kband[0:blk, :] = kp_ref[...]
    kband[blk:, :] = kc_ref[...]
    vband[0:blk, :] = vp_ref[...]
    vband[blk:, :] = vc_ref[...]
    qi = lax.broadcasted_iota(jnp.int32, (blk, 2 * blk), 0)
    ki = lax.broadcasted_iota(jnp.int32, (blk, 2 * blk), 1)
    window = (ki >= qi) & (ki <= qi + blk)
    lane = lax.broadcasted_iota(jnp.int32, (blk, LANES), 1)

    def sub_block(a, carry):
        ro = pl.multiple_of(a * blk, blk)
        first_key = jnp.where(n * rows + ro > 0, 0, blk)
        mask = window & (ki >= first_key)
        lse_tile = jnp.zeros((blk, LANES), F32)
        for hd in range(A_HEADS):
            cs = slice(hd * HEAD_DIM, (hd + 1) * HEAD_DIM)
            s = _dot_nt(q_ref[pl.ds(ro, blk), cs], kband[pl.ds(ro, 2 * blk), cs])
            s = jnp.where(mask, s, NEG)
            m = jnp.max(s, axis=-1, keepdims=True)
            p = jnp.exp2(s - m)
            l = jnp.sum(p, axis=-1, keepdims=True)
            o = _dot(p.astype(BF16), vband[pl.ds(ro, 2 * blk), cs]) / l
            o_ref[pl.ds(ro, blk), cs] = o.astype(o_ref.dtype)
            lse_tile = jnp.where(lane == hd, m + jnp.log2(l), lse_tile)
        lse_ref[pl.ds(ro, blk), :] = lse_tile
        return carry

    lax.fori_loop(0, rows // blk, sub_block, 0)


def _dilated_group(qkv, row_width, batch, seq, window, dilation):
    blk = window // dilation
    sub = seq // dilation
    assert sub % blk == 0 and row_width % A_WIDTH == 0
    rows = _tile(sub, (4 * blk, 2 * blk, blk))
    nb = sub // rows
    rpb = rows // blk
    cpr = row_width // A_WIDTH
    t = batch * seq
    view = qkv.reshape(t // dilation, dilation * row_width)

    def cur(c):
        return pl.BlockSpec((rows, A_WIDTH), lambda b, r, n: (b * nb + n, cpr * r + c))

    def prev(c):
        return pl.BlockSpec((blk, A_WIDTH),
                            lambda b, r, n: (b * (sub // blk) + jnp.maximum(n * rpb - 1, 0), cpr * r + c))

    o, lse = pl.pallas_call(
        functools.partial(_dilated_kernel, rows=rows, blk=blk),
        name=f"dilated_d{dilation}",
        grid=(batch, dilation, nb),
        in_specs=[cur(0), prev(1), cur(1), prev(2), cur(2)],
        out_specs=[pl.BlockSpec((rows, A_WIDTH), lambda b, r, n: (b * nb + n, r)),
                   pl.BlockSpec((rows, LANES), lambda b, r, n: (b * nb + n, r))],
        out_shape=[jax.ShapeDtypeStruct((t // dilation, dilation * A_WIDTH), BF16),
                   jax.ShapeDtypeStruct((t // dilation, dilation * LANES), F32)],
        scratch_shapes=[pltpu.VMEM((rows + blk, A_WIDTH), BF16), pltpu.VMEM((rows + blk, A_WIDTH), BF16)],
        compiler_params=_cparams(("arbitrary", "arbitrary", "arbitrary"),
                                 2 * (3 * rows + 2 * blk) * A_WIDTH * 2 + 2 * rows * (A_WIDTH + LANES) * 4
                                 + 2 * (rows + blk) * A_WIDTH * 2 + 16 * blk * 2 * blk * 4),
    )(view, view, view, view, view)
    return o.reshape(t, A_WIDTH), lse.reshape(t, LANES)


DEINTERLEAVE_ROWS = 16 * max(d for _, d in A_PATTERNS)


def _deinterleave_kernel(x_ref, *refs, dilations):
    out_refs, scr = refs[:-1], refs[-1]
    rows = scr.shape[1]
    for g in range(scr.shape[0]):
        cols = slice(g * LANES, (g + 1) * LANES)
        scr[g] = x_ref[:, cols].astype(F32)
        for o_ref, d in zip(out_refs, dilations):
            for r in range(d):
                o_ref[r, :, cols] = scr[g, pl.ds(r, rows // d, stride=d), :].astype(o_ref.dtype)


def _deinterleave(x, batch, seq, dilations):
    t, w = x.shape
    rows = DEINTERLEAVE_ROWS
    nb = seq // rows
    outs = pl.pallas_call(
        functools.partial(_deinterleave_kernel, dilations=dilations),
        name="dilated_deinterleave",
        grid=(batch, nb),
        in_specs=[pl.BlockSpec((rows, w), lambda b, n: (b * nb + n, 0))],
        out_specs=[pl.BlockSpec((None, d, rows // d, w), lambda b, n: (b, 0, n, 0)) for d in dilations],
        out_shape=[jax.ShapeDtypeStruct((batch, d, seq // d, w), x.dtype) for d in dilations],
        scratch_shapes=[pltpu.VMEM((w // LANES, rows, LANES), F32)],
        compiler_params=_cparams(("arbitrary", "arbitrary"), rows * w * (4 + 4 + 4 * len(dilations) + 8)),
    )(x)
    return [o.reshape(t, w) for o in outs]


def _combine_kernel(*refs, dilations):
    n = len(dilations)
    o_refs, l_refs, out_ref, scratch = refs[:n], refs[n:2 * n], refs[2 * n], refs[2 * n + 1:]
    heads, lses = [], []
    for o_ref, l_ref, d in zip(o_refs, l_refs, dilations):
        if d == 1:
            heads.append(lambda hd, o_ref=o_ref: o_ref[:, hd * HEAD_DIM:(hd + 1) * HEAD_DIM])
            lses.append(l_ref[...])
            continue
        o_sc, l_sc = scratch[:2]
        scratch = scratch[2:]
        rows = o_sc.shape[1]
        for r in range(d):
            dst = pl.ds(r, rows // d, stride=d)
            l_sc[0, dst, :] = l_ref[r]
            for hd in range(A_HEADS):
                o_sc[hd, dst, :] = o_ref[r, :, hd * HEAD_DIM:(hd + 1) * HEAD_DIM].astype(F32)
        heads.append(lambda hd, o_sc=o_sc: o_sc[hd])
        lses.append(l_sc[0])
    mx = functools.reduce(jnp.maximum, lses)
    es = [jnp.exp2(a - mx) for a in lses]
    inv = 1.0 / functools.reduce(jnp.add, es)
    ws = [e * inv for e in es]
    for hd in range(A_HEADS):
        acc = functools.reduce(jnp.add, [w[:, hd:hd + 1] * head(hd) for w, head in zip(ws, heads)])
        out_ref[:, hd * HEAD_DIM:(hd + 1) * HEAD_DIM] = acc.astype(out_ref.dtype)


def _combine(outs, lses, batch, seq, dilations):
    t = batch * seq
    rows = DEINTERLEAVE_ROWS
    nb = seq // rows
    in_specs, args, scratch = [], [], []
    for width, arrs in ((A_WIDTH, outs), (LANES, lses)):
        for a, d in zip(arrs, dilations):
            if d == 1:
                in_specs.append(pl.BlockSpec((rows, width), lambda b, n: (b * nb + n, 0)))
                args.append(a)
            else:
                in_specs.append(pl.BlockSpec((None, d, rows // d, width), lambda b, n: (b, 0, n, 0)))
                args.append(a.reshape(batch, d, seq // d, width))
    for d in dilations:
        if d != 1:
            scratch += [pltpu.VMEM((A_HEADS, rows, LANES), F32), pltpu.VMEM((1, rows, LANES), F32)]
    return pl.pallas_call(
        functools.partial(_combine_kernel, dilations=dilations),
        name="dilated_combine",
        grid=(batch, nb),
        in_specs=in_specs,
        out_specs=pl.BlockSpec((rows, A_WIDTH), lambda b, n: (b * nb + n, 0)),
        out_shape=jax.ShapeDtypeStruct((t, A_WIDTH), BF16),
        scratch_shapes=scratch,
        compiler_params=_cparams(("arbitrary", "arbitrary"),
                                 rows * (A_WIDTH + LANES) * 4 * (3 * len(dilations) + 4)),
    )(*args)


def _merge_kernel(h_ref, oa_ref, ob_ref, oc_ref, wg0, wg1, wg2, wa, wb, wc, o_ref):
    h = h_ref[...]
    acc = jax.nn.sigmoid(_dot(h, wg0[...])) * _dot(oa_ref[...], wa[...])
    acc += jax.nn.sigmoid(_dot(h, wg1[...])) * _dot(ob_ref[...], wb[...])
    acc += jax.nn.sigmoid(_dot(h, wg2[...])) * _dot(oc_ref[...], wc[...])
    o_ref[...] = acc.astype(o_ref.dtype)


def _merge(h, oa, ob, oc, w_gate, w_a, w_b, w_c):
    m, d = h.shape
    bm = _tile(m, (512, 256, 128))
    bn = _tile(d, (512, 256, 128))
    nj = d // bn

    def rows(width):
        return pl.BlockSpec((bm, width), lambda i, j: (i, 0))

    def gate(g):
        return pl.BlockSpec((d, bn), lambda i, j: (0, g * nj + j))

    def branch(width):
        return pl.BlockSpec((width, bn), lambda i, j: (0, j))

    k_all = 3 * d + A_WIDTH + B_WIDTH + C_WIDTH
    return pl.pallas_call(
        _merge_kernel,
        name="gate_merge",
        grid=(m // bm, nj),
        in_specs=[rows(d), rows(A_WIDTH), rows(B_WIDTH), rows(C_WIDTH), gate(0), gate(1), gate(2),
                  branch(A_WIDTH), branch(B_WIDTH), branch(C_WIDTH)],
        out_specs=pl.BlockSpec((bm, bn), lambda i, j: (i, j)),
        out_shape=jax.ShapeDtypeStruct((m, d), BF16),
        compiler_params=_cparams(("arbitrary", "arbitrary"),
                                 2 * (bm * (d + A_WIDTH + B_WIDTH + C_WIDTH) * 2 + k_all * bn * 2 + bm * bn * 2)
                                 + 8 * bm * bn * 4),
    )(h, oa, ob, oc, w_gate, w_gate, w_gate, w_a, w_b, w_c)


def _ffn_in_kernel(h_ref, wa_ref, wb_ref, o_ref):
    h = h_ref[...]
    a = _dot(h, wa_ref[...])
    b = _dot(h, wb_ref[...])
    o_ref[...] = (a * jax.nn.sigmoid(a) * b).astype(o_ref.dtype)


def _ffn_in(h, w_in, d_ff):
    m, d = h.shape
    bn = _tile(d_ff, (512, 256, 128))
    bm = _tile(m, (2048, 1024, 512, 256, 128) if bn <= 256 else (1024, 512, 256, 128))
    nj = d_ff // bn
    return pl.pallas_call(
        _ffn_in_kernel,
        name="ffn_in",
        grid=(m // bm, nj),
        in_specs=[pl.BlockSpec((bm, d), lambda i, j: (i, 0)),
                  pl.BlockSpec((d, bn), lambda i, j: (0, j)),
                  pl.BlockSpec((d, bn), lambda i, j: (0, nj + j))],
        out_specs=pl.BlockSpec((bm, bn), lambda i, j: (i, j)),
        out_shape=jax.ShapeDtypeStruct((m, d_ff), BF16),
        compiler_params=_cparams(("arbitrary", "arbitrary"),
                                 2 * (bm * d * 2 + 2 * d * bn * 2 + bm * bn * 2) + 4 * bm * bn * 4),
    )(h, w_in, w_in)


def _layer(x2, h, ada_l, ada_next, batch, seq, tabs_a, tabs_b, w_in_bf, l, w, alpha, last):
    d = x2.shape[1]
    t = batch * seq
    sh1, sc1, gt1, sh2, sc2, gt2 = [a.reshape(batch, 1, d) for a in jnp.split(ada_l, 6, axis=-1)]
    del sh1, sc1

    qkv_a = _proj_ac(h, w_in_bf, l, SCALE_A, tabs_a)
    qkv_c = _matmul(h, w["c"], BF16, bn_prefs=(1536, 768, 512, 256, 128), scale=SCALE_A, scaled_cols=C_WIDTH)
    q_b = _mla_q(h, w_in_bf, l, w["cq_col0"], w["g_qn"], w["uq"], tabs_b)
    k_b, v_b, f_logit = _mla_kv(h, w["ckv"], w["g_kvn"], w["ukv"], w["b_f"], tabs_b)

    dilations = tuple(dil for _, dil in A_PATTERNS)
    strided = [dil for dil in dilations if dil != 1]
    copies = dict(zip(strided, _deinterleave(qkv_a, batch, seq, strided)))
    outs, lses = [], []
    for window, dil in A_PATTERNS:
        src = qkv_a if dil == 1 else copies[dil]
        o, l = _dilated_group(src, src.shape[1], batch * dil, seq // dil, window // dil, 1)
        outs.append(o)
        lses.append(l)
    o_a = _combine(outs, lses, batch, seq, dilations)

    o_b = _flash(q_b, k_b, v_b, 0, 0, 0, B_QK_PAD, V_DIM, B_HEADS, batch, seq)

    f_t = f_logit[:, :F_ROWS].reshape(batch, seq, F_ROWS).transpose(0, 2, 1)
    c_t = _fox_cumsum(f_t)
    c_row = c_t.reshape(batch * F_ROWS, 1, seq)
    c_rep = jnp.broadcast_to(c_t[:, :C_HEADS, :, None], (batch, C_HEADS, seq, LANES))
    c_rep = c_rep.reshape(batch * C_HEADS, seq, LANES)
    nh = A_WIDTH // HEAD_DIM
    o_c = _flash(qkv_c, qkv_c, qkv_c, 0, nh, 2 * nh, HEAD_DIM, HEAD_DIM, C_HEADS, batch, seq,
                 bias=(c_row, c_rep))

    merged = _merge(h, o_a, o_b, o_c, w["gate"], w["br_a"], w["br_b"], w["br_c"])
    y = _matmul(merged, w["o"], BF16)
    x2, h2 = _res_ln(x2, y, gt1, w["ln1_g"], w["ln1_b"], sc2, sh2, batch, alpha, True)

    act = _ffn_in(h2, w["ffn_in"], w["d_ff"])
    y = _matmul(act, w["ffn_out"], BF16, bm_prefs=(512, 256, 128), bn_prefs=(512, 256, 128))
    if last:
        x2, hn = _res_ln(x2, y, gt2, w["ln2_g"], w["ln2_b"], sc2, sh2, batch, alpha, False)
    else:
        sh1n, sc1n = [a.reshape(batch, 1, d) for a in jnp.split(ada_next, 6, axis=-1)[:2]]
        x2, hn = _res_ln(x2, y, gt2, w["ln2_g"], w["ln2_b"], sc1n, sh1n, batch, alpha, True)
    return x2, hn


SCALE_A = HEAD_DIM ** -0.5 * LOG2E
SCALE_B = (QK_NOPE + QK_ROPE) ** -0.5 * LOG2E


def _prep_weights(l, w_in_bf, b_f, g_qn, w_uq, g_kvn, w_ukv, w_br_a, w_br_b, w_br_c, w_o,
                  ln1_g, ln1_b, w_ffn_in, w_ffn_out, ln2_g, ln2_b):
    d = w_o.shape[1]
    ql = g_qn.shape[1]
    kvl = g_kvn.shape[1]
    wi = w_in_bf[l * d:(l + 1) * d]
    widths = (A_WIDTH, A_WIDTH, A_WIDTH, ql, kvl, QK_ROPE, C_WIDTH, C_WIDTH, C_WIDTH, C_HEADS, N_BRANCH * d)
    offs = [int(o) for o in np.concatenate([[0], np.cumsum(widths)])]
    ckv, kr, fl = [wi[:, offs[i]:offs[i + 1]] for i in (4, 5, 9)]
    zpad = lambda n: jnp.zeros((d, n), BF16)
    w_ckv = jnp.concatenate([ckv, kr, zpad(LANES - QK_ROPE), fl, zpad(LANES - C_HEADS)], axis=1)
    uq = w_uq[l].reshape(ql, B_HEADS, QK_NOPE + QK_ROPE) * SCALE_B
    uq = jnp.pad(uq, ((0, 0), (0, 0), (0, B_QK_PAD - QK_NOPE - QK_ROPE))).reshape(ql, B_HEADS * B_QK_PAD)
    ukv = w_ukv[l].reshape(kvl, B_HEADS, 2, QK_NOPE).transpose(0, 2, 1, 3).reshape(kvl, 2 * B_HEADS * QK_NOPE)
    b_f_row = jnp.pad(b_f[l], (0, LANES - C_HEADS)).reshape(1, LANES)
    return dict(cq_col0=offs[3], c=wi[:, offs[6]:offs[9]], g_qn=g_qn[l], uq=uq.astype(BF16), ckv=w_ckv,
                g_kvn=g_kvn[l], ukv=ukv.astype(BF16), b_f=b_f_row, gate=wi[:, offs[10]:],
                br_a=w_br_a[l].astype(BF16), br_b=w_br_b[l].astype(BF16), br_c=w_br_c[l].astype(BF16),
                o=w_o[l].astype(BF16), ln1_g=ln1_g[l], ln1_b=ln1_b[l], ffn_in=w_ffn_in[l].astype(BF16),
                ffn_out=w_ffn_out[l].astype(BF16), d_ff=w_ffn_out.shape[1], ln2_g=ln2_g[l], ln2_b=ln2_b[l])


def kernel(x, c, positions, w_ada, b_ada, w_in, b_f, g_qn, w_uq, g_kvn, w_ukv, w_br_a, w_br_b, w_br_c, w_o,
           ln1_g, ln1_b, w_ffn_in, w_ffn_out, ln2_g, ln2_b):
    batch, seq, d = x.shape
    depth = w_ada.shape[0]
    alpha = (2.0 * depth) ** 0.25
    t = batch * seq
    x2 = x.reshape(t, d)
    ada = _ada(c, w_ada, b_ada)
    pos_col = positions.astype(F32).reshape(t, 1)
    tabs_a = _rope_tables(pos_col, PARTIAL_ROPE_DIM)
    tabs_b = _rope_tables(pos_col, QK_ROPE)
    sh1, sc1 = [a.reshape(batch, 1, d) for a in jnp.split(ada[0], 6, axis=-1)[:2]]
    h = _modulate(x2, sc1, sh1, batch)
    w_in_bf = w_in.reshape(depth * d, w_in.shape[2]).astype(BF16)
    for l in range(depth):
        w = _prep_weights(l, w_in_bf, b_f, g_qn, w_uq, g_kvn, w_ukv, w_br_a, w_br_b, w_br_c, w_o,
                          ln1_g, ln1_b, w_ffn_in, w_ffn_out, ln2_g, ln2_b)
        last = l == depth - 1
        x2, h = _layer(x2, h, ada[l], None if last else ada[l + 1], batch, seq, tabs_a, tabs_b, w_in_bf, l, w,
                       alpha, last)
    return x2.reshape(batch, seq, d)
```

```python
import functools
import math

import jax
import jax.numpy as jnp
import numpy as np
from jax import lax
from jax.experimental import pallas as pl
from jax.experimental.pallas import tpu as pltpu

HEAD_DIM = 128
ROPE_THETA = 500000.0
PARTIAL_ROPE_DIM = HEAD_DIM // 4
A_HEADS = 12
A_PATTERNS = ((128, 1), (512, 4), (2048, 16))
B_HEADS = 8
QK_NOPE = 128
QK_ROPE = 64
V_DIM = 128
C_HEADS = 12
N_BRANCH = 3
A_WIDTH = A_HEADS * HEAD_DIM
B_WIDTH = B_HEADS * V_DIM
C_WIDTH = C_HEADS * HEAD_DIM
B_QK_PAD = 256
F_ROWS = 16
FLASH_BQ = 4096
FLASH_BK = 1024
NEG = -1e30
LOG2E = math.log2(math.e)
LANES = 128
V7X_VMEM_CAP = 60 * 1024 * 1024

BF16 = jnp.bfloat16
F32 = jnp.float32


def _cparams(semantics, vmem_estimate):
    limit = int(min(max(vmem_estimate * 5 // 4, 32 * 1024 * 1024), V7X_VMEM_CAP))
    return pltpu.CompilerParams(dimension_semantics=semantics, vmem_limit_bytes=limit)


def _tile(n, prefs):
    for p in prefs:
        if n % p == 0:
            return p
    return n


def _resident(shape):
    return pl.BlockSpec(shape, lambda i: (0,) * len(shape), pipeline_mode=pl.Buffered(1))


def _dot(a, b):
    return jnp.dot(a, b, preferred_element_type=F32)


def _dot_nt(a, b):
    return lax.dot_general(a, b, (((1,), (1,)), ((), ())), preferred_element_type=F32)


def _rotate(t, cos, sin_lo, sin_hi, half):
    return t * cos + pltpu.roll(t, half, 1) * sin_hi + pltpu.roll(t, LANES - half, 1) * sin_lo


def _ada_kernel(c_ref, w_ref, b_ref, o_ref, acc_sc):
    kblk = pl.program_id(1)
    nb, kb = c_ref.shape[0], c_ref.shape[1]

    @pl.when(kblk == 0)
    def _():
        acc_sc[...] = jnp.zeros_like(acc_sc)

    cv = c_ref[...]
    s = cv * jax.nn.sigmoid(cv)
    for g in range(w_ref.shape[1] // LANES):
        cols = slice(g * LANES, (g + 1) * LANES)
        wg = w_ref[:, cols]
        for b in range(nb):
            acc_sc[b, :, cols] += jnp.sum((wg * s[b]).reshape(kb // 8, 8, LANES), axis=0)

    @pl.when(kblk == pl.num_programs(1) - 1)
    def _():
        for b in range(nb):
            o_ref[b:b + 1, :] = jnp.sum(acc_sc[b], axis=0, keepdims=True) + b_ref[...]


def _ada(c, w_ada, b_ada):
    depth, d, n = w_ada.shape
    b = c.shape[0]
    kb = _tile(d, (128, 64, 32, 16, 8))
    c_rep = jnp.broadcast_to(c[:, :, None], (b, d, LANES))
    return pl.pallas_call(
        _ada_kernel,
        name="ada",
        grid=(depth, d // kb),
        in_specs=[pl.BlockSpec((b, kb, LANES), lambda l, k: (0, k, 0)),
                  pl.BlockSpec((None, kb, n), lambda l, k: (l, k, 0)),
                  pl.BlockSpec((None, 1, n), lambda l, k: (l, 0, 0))],
        out_specs=pl.BlockSpec((None, b, n), lambda l, k: (l, 0, 0)),
        out_shape=jax.ShapeDtypeStruct((depth, b, n), F32),
        scratch_shapes=[pltpu.VMEM((b, 8, n), F32)],
        compiler_params=_cparams(("arbitrary", "arbitrary"), 2 * kb * n * 4 + 4 * b * 8 * n * 4),
    )(c_rep, w_ada, b_ada.reshape(depth, 1, n))


def _rope_table_kernel(pos_ref, freq_ref, mc_ref, m1_ref, mlo_ref, mhi_ref, cos_ref, lo_ref, hi_ref):
    ang = pos_ref[...] * freq_ref[...]
    cs = jnp.cos(ang)
    sn = jnp.sin(ang)
    cos_ref[...] = cs * mc_ref[...] + m1_ref[...]
    lo_ref[...] = sn * mlo_ref[...]
    hi_ref[...] = sn * mhi_ref[...]


def _rope_tables(pos_col, rot_dim):
    t = pos_col.shape[0]
    half = rot_dim // 2
    inv = np.exp(-math.log(ROPE_THETA) * np.arange(half, dtype=np.float32) * np.float32(2.0 / rot_dim))
    lane = np.arange(LANES)
    freq = np.where(lane < rot_dim, inv[lane % half], 0.0).astype(np.float32)[None]
    m_cos = (lane < rot_dim).astype(np.float32)[None]
    m_one = (lane >= rot_dim).astype(np.float32)[None]
    m_lo = np.where(lane < half, -1.0, 0.0).astype(np.float32)[None]
    m_hi = np.where((lane >= half) & (lane < rot_dim), 1.0, 0.0).astype(np.float32)[None]
    bm = _tile(t, (2048, 1024, 512, 256, 128))
    row = pl.BlockSpec((1, LANES), lambda i: (0, 0))
    tab = pl.BlockSpec((bm, LANES), lambda i: (i, 0))
    shp = jax.ShapeDtypeStruct((t, LANES), F32)
    return pl.pallas_call(
        _rope_table_kernel,
        name="rope_tables",
        grid=(t // bm,),
        in_specs=[pl.BlockSpec((bm, 1), lambda i: (i, 0)), row, row, row, row, row],
        out_specs=[tab, tab, tab],
        out_shape=[shp, shp, shp],
        compiler_params=_cparams(("arbitrary",), 16 * bm * LANES * 4),
    )(pos_col, jnp.asarray(freq), jnp.asarray(m_cos), jnp.asarray(m_one), jnp.asarray(m_lo), jnp.asarray(m_hi))


def _mod_kernel(x_ref, sc_ref, sh_ref, o_ref):
    o_ref[...] = (x_ref[...] * (1.0 + sc_ref[...]) + sh_ref[...]).astype(o_ref.dtype)


def _modulate(x2, sc, sh, batch):
    t, d = x2.shape
    s = t // batch
    bm = _tile(s, (512, 256, 128))
    nb = s // bm
    vec = pl.BlockSpec((None, 1, d), lambda b, i: (b, 0, 0))
    return pl.pallas_call(
        _mod_kernel,
        name="modulate",
        grid=(batch, nb),
        in_specs=[pl.BlockSpec((bm, d), lambda b, i: (b * nb + i, 0)), vec, vec],
        out_specs=pl.BlockSpec((bm, d), lambda b, i: (b * nb + i, 0)),
        out_shape=jax.ShapeDtypeStruct((t, d), BF16),
        compiler_params=_cparams(("arbitrary", "arbitrary"), 2 * bm * d * 6),
    )(x2, sc, sh)


def _res_ln_kernel(x_ref, y_ref, gt_ref, g_ref, b_ref, sc_ref, sh_ref, xo_ref, *ho_ref, alpha):
    z = alpha * x_ref[...] + (1.0 + gt_ref[...]) * y_ref[...].astype(F32)
    mu = jnp.mean(z, axis=-1, keepdims=True)
    zc = z - mu
    var = jnp.mean(zc * zc, axis=-1, keepdims=True)
    xn = zc * lax.rsqrt(var + 1e-5) * g_ref[...] + b_ref[...]
    xo_ref[...] = xn
    if ho_ref:
        ho_ref[0][...] = (xn * (1.0 + sc_ref[...]) + sh_ref[...]).astype(BF16)


def _res_ln(x2, y2, gt, ln_g, ln_b, sc, sh, batch, alpha, with_h):
    t, d = x2.shape
    s = t // batch
    bm = _tile(s, (256, 128))
    nb = s // bm
    vec = pl.BlockSpec((None, 1, d), lambda b, i: (b, 0, 0))
    par = pl.BlockSpec((1, d), lambda b, i: (0, 0))
    blk = pl.BlockSpec((bm, d), lambda b, i: (b * nb + i, 0))
    out_specs = [blk, blk] if with_h else [blk]
    out_shape = [jax.ShapeDtypeStruct((t, d), F32)]
    if with_h:
        out_shape.append(jax.ShapeDtypeStruct((t, d), BF16))
    outs = pl.pallas_call(
        functools.partial(_res_ln_kernel, alpha=alpha),
        name="res_ln",
        grid=(batch, nb),
        in_specs=[blk, blk, vec, par, par, vec, vec],
        out_specs=out_specs,
        out_shape=out_shape,
        compiler_params=_cparams(("arbitrary", "arbitrary"), 2 * bm * d * 14 + 6 * bm * d * 4),
    )(x2, y2, gt, ln_g.reshape(1, d), ln_b.reshape(1, d), sc, sh)
    return (outs[0], outs[1]) if with_h else (outs[0], None)


def _mm_kernel(x_ref, w_ref, o_ref, *, scale, n_scaled):
    acc = _dot(x_ref[...], w_ref[...])
    if n_scaled:
        acc = acc * jnp.where(pl.program_id(1) < n_scaled, scale, 1.0)
    o_ref[...] = acc.astype(o_ref.dtype)


def _matmul(x, w, out_dtype, bm_prefs=(1024, 512, 256, 128), bn_prefs=(1024, 512, 256, 128), scale=1.0,
            scaled_cols=0):
    m, k = x.shape
    n = w.shape[1]
    bm = _tile(m, bm_prefs)
    bn = _tile(n, bn_prefs)
    assert scaled_cols % bn == 0
    osz = jnp.dtype(out_dtype).itemsize
    return pl.pallas_call(
        functools.partial(_mm_kernel, scale=scale, n_scaled=scaled_cols // bn),
        name="matmul",
        grid=(m // bm, n // bn),
        in_specs=[pl.BlockSpec((bm, k), lambda i, j: (i, 0)),
                  pl.BlockSpec((k, bn), lambda i, j: (0, j))],
        out_specs=pl.BlockSpec((bm, bn), lambda i, j: (i, j)),
        out_shape=jax.ShapeDtypeStruct((m, n), out_dtype),
        compiler_params=_cparams(("arbitrary", "arbitrary"),
                                 2 * (bm * k * 2 + k * bn * 2 + bm * bn * osz) + bm * bn * 4),
    )(x, w)


def _proj_ac_kernel(x_ref, w_ref, cos_ref, lo_ref, hi_ref, o_ref, *, n_rot_tiles, half, scale):
    j = pl.program_id(1)
    acc = _dot(x_ref[...], w_ref[...]) * jnp.where(j < n_rot_tiles // 2, scale, 1.0)

    @pl.when(j < n_rot_tiles)
    def _():
        cs, lo, hi = cos_ref[...], lo_ref[...], hi_ref[...]
        for g in range(acc.shape[1] // LANES):
            sl = slice(g * LANES, (g + 1) * LANES)
            o_ref[:, sl] = _rotate(acc[:, sl], cs, lo, hi, half).astype(o_ref.dtype)

    @pl.when(j >= n_rot_tiles)
    def _():
        o_ref[...] = acc.astype(o_ref.dtype)


def _proj_ac(h, w_in_bf, l, scale, tabs):
    m, k = h.shape
    n = 3 * A_WIDTH
    bm = _tile(m, (1024, 512, 256, 128))
    bn = _tile(A_WIDTH, (768, 512, 256, 128))
    tab = pl.BlockSpec((bm, LANES), lambda i, j: (i, 0))
    return pl.pallas_call(
        functools.partial(_proj_ac_kernel, n_rot_tiles=2 * A_WIDTH // bn, half=PARTIAL_ROPE_DIM // 2, scale=scale),
        name="proj_ac",
        grid=(m // bm, n // bn),
        in_specs=[pl.BlockSpec((bm, k), lambda i, j: (i, 0)),
                  pl.BlockSpec((k, bn), lambda i, j: (l, j)), tab, tab, tab],
        out_specs=pl.BlockSpec((bm, bn), lambda i, j: (i, j)),
        out_shape=jax.ShapeDtypeStruct((m, n), BF16),
        compiler_params=_cparams(("arbitrary", "arbitrary"),
                                 2 * (bm * k * 2 + k * bn * 2 + bm * bn * 2 + 3 * bm * LANES * 4) + 2 * bm * bn * 4),
    )(h, w_in_bf, *tabs)


def _rms(x, g, eps=1e-6):
    return x * lax.rsqrt(jnp.mean(x * x, axis=-1, keepdims=True) + eps) * g


def _mla_q_kernel(h_ref, wcq_ref, g_ref, wuq_ref, cos_ref, lo_ref, hi_ref, o_ref):
    cq = _dot(h_ref[...], wcq_ref[...])
    q = _dot(_rms(cq, g_ref[...]).astype(BF16), wuq_ref[...])
    cs, lo, hi = cos_ref[...], lo_ref[...], hi_ref[...]
    for hd in range(B_HEADS):
        base = hd * B_QK_PAD
        o_ref[:, base:base + QK_NOPE] = q[:, base:base + QK_NOPE].astype(BF16)
        rope = _rotate(q[:, base + QK_NOPE:base + B_QK_PAD], cs, lo, hi, QK_ROPE // 2)
        o_ref[:, base + QK_NOPE:base + B_QK_PAD] = rope.astype(BF16)


def _mla_q(h, w_in_bf, l, cq_col0, g_qn, w_uq, tabs):
    m, k = h.shape
    ql = g_qn.shape[0]
    n = w_uq.shape[1]
    assert cq_col0 % ql == 0
    bm = _tile(m, (512, 256, 128))
    tab = pl.BlockSpec((bm, LANES), lambda i: (i, 0))
    return pl.pallas_call(
        _mla_q_kernel,
        name="mla_q",
        grid=(m // bm,),
        in_specs=[pl.BlockSpec((bm, k), lambda i: (i, 0)),
                  pl.BlockSpec((k, ql), lambda i: (l, cq_col0 // ql), pipeline_mode=pl.Buffered(1)),
                  _resident((1, ql)), _resident((ql, n)), tab, tab, tab],
        out_specs=pl.BlockSpec((bm, n), lambda i: (i, 0)),
        out_shape=jax.ShapeDtypeStruct((m, n), BF16),
        compiler_params=_cparams(("arbitrary",),
                                 2 * (bm * k * 2 + bm * n * 2) + k * ql * 2 + ql * n * 2 + bm * (ql + n) * 8),
    )(h, w_in_bf, g_qn.reshape(1, ql), w_uq, *tabs)


def _mla_kv_kernel(h_ref, wc_ref, g_ref, wukv_ref, bf_ref, cos_ref, lo_ref, hi_ref, k_ref, v_ref, f_ref, *, kvl):
    ck = _dot(h_ref[...], wc_ref[...])
    kv = _dot(_rms(ck[:, :kvl], g_ref[...]).astype(BF16), wukv_ref[...])
    kr = _rotate(ck[:, kvl:kvl + LANES], cos_ref[...], lo_ref[...], hi_ref[...], QK_ROPE // 2).astype(BF16)
    for hd in range(B_HEADS):
        base = hd * B_QK_PAD
        k_ref[:, base:base + QK_NOPE] = kv[:, hd * QK_NOPE:(hd + 1) * QK_NOPE].astype(BF16)
        k_ref[:, base + QK_NOPE:base + B_QK_PAD] = kr
    v_ref[...] = kv[:, B_HEADS * QK_NOPE:].astype(BF16)
    f_ref[...] = ck[:, kvl + LANES:] + bf_ref[...]


def _mla_kv(h, w_c, g_kvn, w_ukv, b_f_row, tabs):
    m, k = h.shape
    kvl = g_kvn.shape[0]
    nc = w_c.shape[1]
    bm = _tile(m, (512, 256, 128))
    tab = pl.BlockSpec((bm, LANES), lambda i: (i, 0))
    nk = B_HEADS * B_QK_PAD
    return pl.pallas_call(
        functools.partial(_mla_kv_kernel, kvl=kvl),
        name="mla_kv",
        grid=(m // bm,),
        in_specs=[pl.BlockSpec((bm, k), lambda i: (i, 0)),
                  _resident((k, nc)), _resident((1, kvl)), _resident(w_ukv.shape), _resident((1, LANES)),
                  tab, tab, tab],
        out_specs=[pl.BlockSpec((bm, nk), lambda i: (i, 0)),
                   pl.BlockSpec((bm, B_WIDTH), lambda i: (i, 0)),
                   pl.BlockSpec((bm, LANES), lambda i: (i, 0))],
        out_shape=[jax.ShapeDtypeStruct((m, nk), BF16),
                   jax.ShapeDtypeStruct((m, B_WIDTH), BF16),
                   jax.ShapeDtypeStruct((m, LANES), F32)],
        compiler_params=_cparams(("arbitrary",),
                                 2 * (bm * k * 2 + k * nc * 2 + w_ukv.size * 2 + bm * (nk + B_WIDTH) * 2)
                                 + bm * (nc + nk + B_WIDTH) * 8),
    )(h, w_c, g_kvn.reshape(1, kvl), w_ukv, b_f_row, *tabs)


def _fox_cumsum_kernel(f_ref, o_ref):
    x = f_ref[...]
    y = (jnp.minimum(x, 0.0) - jnp.log(1.0 + jnp.exp(-jnp.abs(x)))) * LOG2E
    s = y.shape[1]
    lane = lax.broadcasted_iota(jnp.int32, y.shape, 1)
    shift = 1
    while shift < s:
        y = y + jnp.where(lane >= shift, pltpu.roll(y, shift, 1), 0.0)
        shift *= 2
    o_ref[...] = y


def _fox_cumsum(f_t):
    b, r, s = f_t.shape
    return pl.pallas_call(
        _fox_cumsum_kernel,
        name="fox_cumsum",
        grid=(b,),
        in_specs=[pl.BlockSpec((None, r, s), lambda i: (i, 0, 0))],
        out_specs=pl.BlockSpec((None, r, s), lambda i: (i, 0, 0)),
        out_shape=jax.ShapeDtypeStruct((b, r, s), F32),
        compiler_params=_cparams(("arbitrary",), 8 * r * s * 4),
    )(f_t)


def _flash_kernel(*refs, bq, bk, has_bias):
    if has_bias:
        q_ref, k_ref, v_ref, cq_ref, ck_ref, o_ref, acc_sc = refs
    else:
        q_ref, k_ref, v_ref, o_ref, acc_sc = refs
    n_diag = bq // bk
    n_full = pl.program_id(2) * n_diag

    def step(j, m, l, q0, masked):
        nq = bq - q0
        start = pl.multiple_of(j * bk, bk)
        s = _dot_nt(k_ref[pl.ds(start, bk), :], q_ref[q0:, :])
        if has_bias:
            ck = ck_ref[pl.ds(start, bk), :]
            s = s + cq_ref[:, q0:] - jnp.concatenate([ck] * (nq // LANES), axis=1)
        if masked:
            key = lax.broadcasted_iota(jnp.int32, (bk, nq), 0)
            qry = lax.broadcasted_iota(jnp.int32, (bk, nq), 1)
            s = jnp.where(key <= qry, s, NEG)
        m_prev = m[:, q0:]
        m_new = jnp.maximum(m_prev, jnp.max(s, axis=0, keepdims=True))
        alpha = jnp.exp2(m_prev - m_new)
        p = jnp.exp2(s - m_new)
        l_new = alpha * l[:, q0:] + jnp.sum(p, axis=0, keepdims=True)
        pv = lax.dot_general(v_ref[pl.ds(start, bk), :], p.astype(BF16), (((0,), (0,)), ((), ())),
                             preferred_element_type=F32)
        acc_sc[:, q0:] = alpha * acc_sc[:, q0:] + pv
        if q0:
            m_new = jnp.concatenate([m[:, :q0], m_new], axis=1)
            l_new = jnp.concatenate([l[:, :q0], l_new], axis=1)
        return m_new, l_new

    acc_sc[...] = jnp.zeros_like(acc_sc)
    init = (jnp.full((1, bq), NEG, F32), jnp.zeros((1, bq), F32))
    m, l = lax.fori_loop(0, n_full, lambda j, c: step(j, c[0], c[1], 0, False), init)
    for t in range(n_diag):
        m, l = step(n_full + t, m, l, t * bk, True)
    o_ref[...] = jnp.transpose(acc_sc[...] / l).astype(o_ref.dtype)


def _flash(q_arr, k_arr, v_arr, q_col0, k_col0, v_col0, dq, dv, heads, batch, seq, bias=None):
    bq = _tile(seq, (FLASH_BQ, 2048, 1024, 512, 256, 128))
    bk = _tile(bq, (FLASH_BK, 512, 256, 128))
    nq = seq // bq
    t = batch * seq
    in_specs = [pl.BlockSpec((bq, dq), lambda b, h, i: (b * nq + i, q_col0 + h)),
                pl.BlockSpec((seq, dq), lambda b, h, i: (b, k_col0 + h)),
                pl.BlockSpec((seq, dv), lambda b, h, i: (b, v_col0 + h))]
    args = [q_arr, k_arr, v_arr]
    if bias is not None:
        c_row, c_rep = bias
        in_specs += [pl.BlockSpec((None, 1, bq), lambda b, h, i: (b * F_ROWS + h, 0, i)),
                     pl.BlockSpec((None, seq, LANES), lambda b, h, i: (b * heads + h, 0, 0))]
        args += [c_row, c_rep]
    return pl.pallas_call(
        functools.partial(_flash_kernel, bq=bq, bk=bk, has_bias=bias is not None),
        name="flash_fox" if bias is not None else "flash_mla",
        grid=(batch, heads, nq),
        in_specs=in_specs,
        out_specs=pl.BlockSpec((bq, dv), lambda b, h, i: (b * nq + i, h)),
        out_shape=jax.ShapeDtypeStruct((t, heads * dv), BF16),
        scratch_shapes=[pltpu.VMEM((dv, bq), F32)],
        compiler_params=_cparams(("arbitrary", "arbitrary", "arbitrary"),
                                 2 * (seq * (dq + dv) * 2 + bq * (dq + dv) * 2 + seq * LANES * 4)
                                 + 4 * bq * bk * 4),
    )(*args)


def _dilated_kernel(q_ref, kp_ref, kc_ref, vp_ref, vc_ref, o_ref, lse_ref, kband, vband, *, rows, blk):
    n = pl.program_id(2)
    kband[0:blk, :] = kp_ref[...]
    kband[blk:, :] = kc_ref[...]
    vband[0:blk, :] = vp_ref[...]
    vband[blk:, :] = vc_ref[...]
    qi = lax.broadcasted_iota(jnp.int32, (blk, 2 * blk), 0)
    ki = lax.broadcasted_iota(jnp.int32, (blk, 2 * blk), 1)
    window = (ki >= qi) & (ki <= qi + blk)
    lane = lax.broadcasted_iota(jnp.int32, (blk, LANES), 1)

    def sub_block(a, carry):
        ro = pl.multiple_of(a * blk, blk)
        first_key = jnp.where(n * rows + ro > 0, 0, blk)
        mask = window & (ki >= first_key)
        lse_tile = jnp.zeros((blk, LANES), F32)
        for hd in range(A_HEADS):
            cs = slice(hd * HEAD_DIM, (hd + 1) * HEAD_DIM)
            s = _dot_nt(q_ref[pl.ds(ro, blk), cs], kband[pl.ds(ro, 2 * blk), cs])
            s = jnp.where(mask, s, NEG)
            m = jnp.max(s, axis=-1, keepdims=True)
            p = jnp.exp2(s - m)
            l = jnp.sum(p, axis=-1, keepdims=True)
            o = _dot(p.astype(BF16), vband[pl.ds(ro, 2 * blk), cs]) / l
            o_ref[pl.ds(ro, blk), cs] = o.astype(o_ref.dtype)
            lse_tile = jnp.where(lane == hd, m + jnp.log2(l), lse_tile)
        lse_ref[pl.ds(ro, blk), :] = lse_tile
        return carry

    lax.fori_loop(0, rows // blk, sub_block, 0)


def _dilated_group(qkv, row_width, batch, seq, window, dilation):
    blk = window // dilation
    sub = seq // dilation
    assert sub % blk == 0 and row_width % A_WIDTH == 0
    rows = _tile(sub, (4 * blk, 2 * blk, blk))
    nb = sub // rows
    rpb = rows // blk
    cpr = row_width // A_WIDTH
    t = batch * seq
    view = qkv.reshape(t // dilation, dilation * row_width)

    def cur(c):
        return pl.BlockSpec((rows, A_WIDTH), lambda b, r, n: (b * nb + n, cpr * r + c))

    def prev(c):
        return pl.BlockSpec((blk, A_WIDTH),
                            lambda b, r, n: (b * (sub // blk) + jnp.maximum(n * rpb - 1, 0), cpr * r + c))

    o, lse = pl.pallas_call(
        functools.partial(_dilated_kernel, rows=rows, blk=blk),
        name=f"dilated_d{dilation}",
        grid=(batch, dilation, nb),
        in_specs=[cur(0), prev(1), cur(1), prev(2), cur(2)],
        out_specs=[pl.BlockSpec((rows, A_WIDTH), lambda b, r, n: (b * nb + n, r)),
                   pl.BlockSpec((rows, LANES), lambda b, r, n: (b * nb + n, r))],
        out_shape=[jax.ShapeDtypeStruct((t // dilation, dilation * A_WIDTH), BF16),
                   jax.ShapeDtypeStruct((t // dilation, dilation * LANES), F32)],
        scratch_shapes=[pltpu.VMEM((rows + blk, A_WIDTH), BF16), pltpu.VMEM((rows + blk, A_WIDTH), BF16)],
        compiler_params=_cparams(("arbitrary", "arbitrary", "arbitrary"),
                                 2 * (3 * rows + 2 * blk) * A_WIDTH * 2 + 2 * rows * (A_WIDTH + LANES) * 4
                                 + 2 * (rows + blk) * A_WIDTH * 2 + 16 * blk * 2 * blk * 4),
    )(view, view, view, view, view)
    return o.reshape(t, A_WIDTH), lse.reshape(t, LANES)


DEINTERLEAVE_ROWS = 16 * max(d for _, d in A_PATTERNS)


def _deinterleave_kernel(x_ref, *refs, dilations):
    out_refs, scr = refs[:-1], refs[-1]
    rows = scr.shape[1]
    for g in range(scr.shape[0]):
        cols = slice(g * LANES, (g + 1) * LANES)
        scr[g] = x_ref[:, cols].astype(F32)
        for o_ref, d in zip(out_refs, dilations):
            for r in range(d):
                o_ref[r, :, cols] = scr[g, pl.ds(r, rows // d, stride=d), :].astype(o_ref.dtype)


def _deinterleave(x, batch, seq, dilations):
    t, w = x.shape
    rows = DEINTERLEAVE_ROWS
    nb = seq // rows
    outs = pl.pallas_call(
        functools.partial(_deinterleave_kernel, dilations=dilations),
        name="dilated_deinterleave",
        grid=(batch, nb),
        in_specs=[pl.BlockSpec((rows, w), lambda b, n: (b * nb + n, 0))],
        out_specs=[pl.BlockSpec((None, d, rows // d, w), lambda b, n: (b, 0, n, 0)) for d in dilations],
        out_shape=[jax.ShapeDtypeStruct((batch, d, seq // d, w), x.dtype) for d in dilations],
        scratch_shapes=[pltpu.VMEM((w // LANES, rows, LANES), F32)],
        compiler_params=_cparams(("arbitrary", "arbitrary"), rows * w * (4 + 4 + 4 * len(dilations) + 8)),
    )(x)
    return [o.reshape(t, w) for o in outs]


def _combine_kernel(*refs, dilations):
    n = len(dilations)
    o_refs, l_refs, out_ref, scratch = refs[:n], refs[n:2 * n], refs[2 * n], refs[2 * n + 1:]
    heads, lses = [], []
    for o_ref, l_ref, d in zip(o_refs, l_refs, dilations):
        if d == 1:
            heads.append(lambda hd, o_ref=o_ref: o_ref[:, hd * HEAD_DIM:(hd + 1) * HEAD_DIM])
            lses.append(l_ref[...])
            continue
        o_sc, l_sc = scratch[:2]
        scratch = scratch[2:]
        rows = o_sc.shape[1]
        for r in range(d):
            dst = pl.ds(r, rows // d, stride=d)
            l_sc[0, dst, :] = l_ref[r]
            for hd in range(A_HEADS):
                o_sc[hd, dst, :] = o_ref[r, :, hd * HEAD_DIM:(hd + 1) * HEAD_DIM].astype(F32)
        heads.append(lambda hd, o_sc=o_sc: o_sc[hd])
        lses.append(l_sc[0])
    mx = functools.reduce(jnp.maximum, lses)
    es = [jnp.exp2(a - mx) for a in lses]
    inv = 1.0 / functools.reduce(jnp.add, es)
    ws = [e * inv for e in es]
    for hd in range(A_HEADS):
        acc = functools.reduce(jnp.add, [w[:, hd:hd + 1] * head(hd) for w, head in zip(ws, heads)])
        out_ref[:, hd * HEAD_DIM:(hd + 1) * HEAD_DIM] = acc.astype(out_ref.dtype)


def _combine(outs, lses, batch, seq, dilations):
    t = batch * seq
    rows = DEINTERLEAVE_ROWS
    nb = seq // rows
    in_specs, args, scratch = [], [], []
    for width, arrs in ((A_WIDTH, outs), (LANES, lses)):
        for a, d in zip(arrs, dilations):
            if d == 1:
                in_specs.append(pl.BlockSpec((rows, width), lambda b, n: (b * nb + n, 0)))
                args.append(a)
            else:
                in_specs.append(pl.BlockSpec((None, d, rows // d, width), lambda b, n: (b, 0, n, 0)))
                args.append(a.reshape(batch, d, seq // d, width))
    for d in dilations:
        if d != 1:
            scratch += [pltpu.VMEM((A_HEADS, rows, LANES), F32), pltpu.VMEM((1, rows, LANES), F32)]
    return pl.pallas_call(
        functools.partial(_combine_kernel, dilations=dilations),
        name="dilated_combine",
        grid=(batch, nb),
        in_specs=in_specs,
        out_specs=pl.BlockSpec((rows, A_WIDTH), lambda b, n: (b * nb + n, 0)),
        out_shape=jax.ShapeDtypeStruct((t, A_WIDTH), BF16),
        scratch_shapes=scratch,
        compiler_params=_cparams(("arbitrary", "arbitrary"),
                                 rows * (A_WIDTH + LANES) * 4 * (3 * len(dilations) + 4)),
    )(*args)


def _merge_kernel(h_ref, oa_ref, ob_ref, oc_ref, wg0, wg1, wg2, wa, wb, wc, o_ref):
    h = h_ref[...]
    acc = jax.nn.sigmoid(_dot(h, wg0[...])) * _dot(oa_ref[...], wa[...])
    acc += jax.nn.sigmoid(_dot(h, wg1[...])) * _dot(ob_ref[...], wb[...])
    acc += jax.nn.sigmoid(_dot(h, wg2[...])) * _dot(oc_ref[...], wc[...])
    o_ref[...] = acc.astype(o_ref.dtype)


def _merge(h, oa, ob, oc, w_gate, w_a, w_b, w_c):
    m, d = h.shape
    bm = _tile(m, (512, 256, 128))
    bn = _tile(d, (512, 256, 128))
    nj = d // bn

    def rows(width):
        return pl.BlockSpec((bm, width), lambda i, j: (i, 0))

    def gate(g):
        return pl.BlockSpec((d, bn), lambda i, j: (0, g * nj + j))

    def branch(width):
        return pl.BlockSpec((width, bn), lambda i, j: (0, j))

    k_all = 3 * d + A_WIDTH + B_WIDTH + C_WIDTH
    return pl.pallas_call(
        _merge_kernel,
        name="gate_merge",
        grid=(m // bm, nj),
        in_specs=[rows(d), rows(A_WIDTH), rows(B_WIDTH), rows(C_WIDTH), gate(0), gate(1), gate(2),
                  branch(A_WIDTH), branch(B_WIDTH), branch(C_WIDTH)],
        out_specs=pl.BlockSpec((bm, bn), lambda i, j: (i, j)),
        out_shape=jax.ShapeDtypeStruct((m, d), BF16),
        compiler_params=_cparams(("arbitrary", "arbitrary"),
                                 2 * (bm * (d + A_WIDTH + B_WIDTH + C_WIDTH) * 2 + k_all * bn * 2 + bm * bn * 2)
                                 + 8 * bm * bn * 4),
    )(h, oa, ob, oc, w_gate, w_gate, w_gate, w_a, w_b, w_c)


def _ffn_in_kernel(h_ref, wa_ref, wb_ref, o_ref):
    h = h_ref[...]
    a = _dot(h, wa_ref[...].astype(BF16))
    b = _dot(h, wb_ref[...].astype(BF16))
    o_ref[...] = (a * jax.nn.sigmoid(a) * b).astype(o_ref.dtype)


def _ffn_in(h, w_ffn_in, l, d_ff):
    m, d = h.shape
    bn = _tile(d_ff, (256, 128))
    bm = _tile(m, (2048, 1024, 512, 256, 128))
    nj = d_ff // bn
    return pl.pallas_call(
        _ffn_in_kernel,
        name="ffn_in",
        grid=(m // bm, nj),
        in_specs=[pl.BlockSpec((bm, d), lambda i, j: (i, 0)),
                  pl.BlockSpec((None, d, bn), lambda i, j: (l, 0, j)),
                  pl.BlockSpec((None, d, bn), lambda i, j: (l, 0, nj + j))],
        out_specs=pl.BlockSpec((bm, bn), lambda i, j: (i, j)),
        out_shape=jax.ShapeDtypeStruct((m, d_ff), BF16),
        compiler_params=_cparams(("arbitrary", "arbitrary"),
                                 2 * (bm * d * 2 + 2 * d * bn * 4 + bm * bn * 2) + 2 * d * bn * 2 + 4 * bm * bn * 4),
    )(h, w_ffn_in, w_ffn_in)


def _layer(x2, h, ada_l, ada_next, batch, seq, tabs_a, tabs_b, w_in_bf, l, w, alpha, last):
    d = x2.shape[1]
    t = batch * seq
    sh1, sc1, gt1, sh2, sc2, gt2 = [a.reshape(batch, 1, d) for a in jnp.split(ada_l, 6, axis=-1)]
    del sh1, sc1

    qkv_a = _proj_ac(h, w_in_bf, l, SCALE_A, tabs_a)
    qkv_c = _matmul(h, w["c"], BF16, bn_prefs=(1536, 768, 512, 256, 128), scale=SCALE_A, scaled_cols=C_WIDTH)
    q_b = _mla_q(h, w_in_bf, l, w["cq_col0"], w["g_qn"], w["uq"], tabs_b)
    k_b, v_b, f_logit = _mla_kv(h, w["ckv"], w["g_kvn"], w["ukv"], w["b_f"], tabs_b)

    dilations = tuple(dil for _, dil in A_PATTERNS)
    strided = [dil for dil in dilations if dil != 1]
    copies = dict(zip(strided, _deinterleave(qkv_a, batch, seq, strided)))
    outs, lses = [], []
    for window, dil in A_PATTERNS:
        src = qkv_a if dil == 1 else copies[dil]
        o, lse = _dilated_group(src, src.shape[1], batch * dil, seq // dil, window // dil, 1)
        outs.append(o)
        lses.append(lse)
    o_a = _combine(outs, lses, batch, seq, dilations)

    o_b = _flash(q_b, k_b, v_b, 0, 0, 0, B_QK_PAD, V_DIM, B_HEADS, batch, seq)

    f_t = f_logit[:, :F_ROWS].reshape(batch, seq, F_ROWS).transpose(0, 2, 1)
    c_t = _fox_cumsum(f_t)
    c_row = c_t.reshape(batch * F_ROWS, 1, seq)
    c_rep = jnp.broadcast_to(c_t[:, :C_HEADS, :, None], (batch, C_HEADS, seq, LANES))
    c_rep = c_rep.reshape(batch * C_HEADS, seq, LANES)
    nh = A_WIDTH // HEAD_DIM
    o_c = _flash(qkv_c, qkv_c, qkv_c, 0, nh, 2 * nh, HEAD_DIM, HEAD_DIM, C_HEADS, batch, seq,
                 bias=(c_row, c_rep))

    merged = _merge(h, o_a, o_b, o_c, w["gate"], w["br_a"], w["br_b"], w["br_c"])
    y = _matmul(merged, w["o"], BF16)
    x2, h2 = _res_ln(x2, y, gt1, w["ln1_g"], w["ln1_b"], sc2, sh2, batch, alpha, True)

    act = _ffn_in(h2, w["ffn_in"], l, w["d_ff"])
    y = _matmul(act, w["ffn_out"], BF16, bm_prefs=(512, 256, 128), bn_prefs=(512, 256, 128))
    if last:
        x2, hn = _res_ln(x2, y, gt2, w["ln2_g"], w["ln2_b"], sc2, sh2, batch, alpha, False)
    else:
        sh1n, sc1n = [a.reshape(batch, 1, d) for a in jnp.split(ada_next, 6, axis=-1)[:2]]
        x2, hn = _res_ln(x2, y, gt2, w["ln2_g"], w["ln2_b"], sc1n, sh1n, batch, alpha, True)
    return x2, hn


SCALE_A = HEAD_DIM ** -0.5 * LOG2E
SCALE_B = (QK_NOPE + QK_ROPE) ** -0.5 * LOG2E


def _prep_weights(l, w_in_bf, b_f, g_qn, w_uq, g_kvn, w_ukv, w_br_a, w_br_b, w_br_c, w_o,
                  ln1_g, ln1_b, w_ffn_in, w_ffn_out, ln2_g, ln2_b):
    d = w_o.shape[1]
    ql = g_qn.shape[1]
    kvl = g_kvn.shape[1]
    wi = w_in_bf[l * d:(l + 1) * d]
    widths = (A_WIDTH, A_WIDTH, A_WIDTH, ql, kvl, QK_ROPE, C_WIDTH, C_WIDTH, C_WIDTH, C_HEADS, N_BRANCH * d)
    offs = [int(o) for o in np.concatenate([[0], np.cumsum(widths)])]
    ckv, kr, fl = [wi[:, offs[i]:offs[i + 1]] for i in (4, 5, 9)]
    zpad = lambda n: jnp.zeros((d, n), BF16)
    w_ckv = jnp.concatenate([ckv, kr, zpad(LANES - QK_ROPE), fl, zpad(LANES - C_HEADS)], axis=1)
    uq = w_uq[l].reshape(ql, B_HEADS, QK_NOPE + QK_ROPE) * SCALE_B
    uq = jnp.pad(uq, ((0, 0), (0, 0), (0, B_QK_PAD - QK_NOPE - QK_ROPE))).reshape(ql, B_HEADS * B_QK_PAD)
    ukv = w_ukv[l].reshape(kvl, B_HEADS, 2, QK_NOPE).transpose(0, 2, 1, 3).reshape(kvl, 2 * B_HEADS * QK_NOPE)
    b_f_row = jnp.pad(b_f[l], (0, LANES - C_HEADS)).reshape(1, LANES)
    return dict(cq_col0=offs[3], c=wi[:, offs[6]:offs[9]], g_qn=g_qn[l], uq=uq.astype(BF16), ckv=w_ckv,
                g_kvn=g_kvn[l], ukv=ukv.astype(BF16), b_f=b_f_row, gate=wi[:, offs[10]:],
                br_a=w_br_a[l].astype(BF16), br_b=w_br_b[l].astype(BF16), br_c=w_br_c[l].astype(BF16),
                o=w_o[l].astype(BF16), ln1_g=ln1_g[l], ln1_b=ln1_b[l], ffn_in=w_ffn_in,
                ffn_out=w_ffn_out[l].astype(BF16), d_ff=w_ffn_out.shape[1], ln2_g=ln2_g[l], ln2_b=ln2_b[l])


def kernel(x, c, positions, w_ada, b_ada, w_in, b_f, g_qn, w_uq, g_kvn, w_ukv, w_br_a, w_br_b, w_br_c, w_o,
           ln1_g, ln1_b, w_ffn_in, w_ffn_out, ln2_g, ln2_b):
    batch, seq, d = x.shape
    depth = w_ada.shape[0]
    alpha = (2.0 * depth) ** 0.25
    t = batch * seq
    x2 = x.reshape(t, d)
    ada = _ada(c, w_ada, b_ada)
    pos_col = positions.astype(F32).reshape(t, 1)
    tabs_a = _rope_tables(pos_col, PARTIAL_ROPE_DIM)
    tabs_b = _rope_tables(pos_col, QK_ROPE)
    sh1, sc1 = [a.reshape(batch, 1, d) for a in jnp.split(ada[0], 6, axis=-1)[:2]]
    h = _modulate(x2, sc1, sh1, batch)
    w_in_bf = w_in.reshape(depth * d, w_in.shape[2]).astype(BF16)
    for l in range(depth):
        w = _prep_weights(l, w_in_bf, b_f, g_qn, w_uq, g_kvn, w_ukv, w_br_a, w_br_b, w_br_c, w_o,
                          ln1_g, ln1_b, w_ffn_in, w_ffn_out, ln2_g, ln2_b)
        last = l == depth - 1
        x2, h = _layer(x2, h, ada[l], None if last else ada[l + 1], batch, seq, tabs_a, tabs_b, w_in_bf, l, w,
                       alpha, last)
    return x2.reshape(batch, seq, d)
```
